```python
import math
import jax, jax.numpy as jnp
from jax import lax
import numpy as np

D_MODEL = 1024
BATCH = 8
SEQ = 4096
DEPTH = 2

D_G = D_MODEL // 4
N_GROUPS = 5
D_MIX = N_GROUPS * D_G
N_SUB = 4
HEAD_DIM = D_G // N_SUB
CONV_A = 3
CONV_D = 31
CHUNK = 128
POOL_WINDOWS = (2, 4, 8, 16)
MEM_LEN = 256
LN_EPS = 1e-5
DEEPNORM_ALPHA = (2.0 * DEPTH) ** 0.25
DEEPNORM_BETA = (8.0 * DEPTH) ** -0.25

SPLIT_SIZES = (D_G, D_G, D_G, D_G, D_G, D_G, D_G, D_G, D_G, D_MIX)
D_IN = sum(SPLIT_SIZES)
SPLIT_OFFSETS = tuple(int(o) for o in np.cumsum(SPLIT_SIZES)[:-1])

kernel_name = "hybrid_parallel_conv_sgu_pool_conformer_memxattn"


def layer_norm(x, g, b):
    xf = x.astype(jnp.float32)
    mu = jnp.mean(xf, axis=-1, keepdims=True)
    var = jnp.mean(jnp.square(xf - mu), axis=-1, keepdims=True)
    y = (xf - mu) * lax.rsqrt(var + LN_EPS) * g.astype(jnp.float32) + b.astype(jnp.float32)
    return y.astype(x.dtype)


def causal_depthwise_conv(x, w):
    k, c = w.shape
    return lax.conv_general_dilated(
        x, w[:, None, :].astype(x.dtype), window_strides=(1,), padding=[(k - 1, 0)],
        dimension_numbers=("NWC", "WIO", "NWC"), feature_group_count=c)


def short_gated_conv(xa, ba, ca, w_conv):
    return ba * causal_depthwise_conv(ca * xa, w_conv)


def spatial_gating(u, v, ln_g, ln_b, w_s, b_s):
    bn, s, _ = v.shape
    u = jax.nn.gelu(u)
    v = layer_norm(jax.nn.gelu(v), ln_g, ln_b)
    v = v.reshape(bn, s // CHUNK, CHUNK, N_SUB, HEAD_DIM)
    mask = jnp.tril(jnp.ones((CHUNK, CHUNK), dtype=bool))
    w = jnp.where(mask[None], w_s, jnp.zeros_like(w_s))
    mixed = jnp.einsum("hts,bcshd->bcthd", w, v) + b_s.T[:, :, None]
    return u * mixed.reshape(bn, s, D_G)


def multiscale_pool(xc, w_grp, scale):
    bn, s, _ = xc.shape
    xf = xc.astype(jnp.float32).reshape(bn, s, N_SUB, HEAD_DIM)
    cs = jnp.pad(jnp.cumsum(xf, axis=1), ((0, 0), (1, 0), (0, 0), (0, 0)))
    t = jnp.arange(s)
    win = jnp.array(POOL_WINDOWS, dtype=jnp.int32)
    lo = jnp.maximum(t[:, None] + 1 - win[None, :], 0)
    window_sum = cs[:, 1:] - cs[:, lo, jnp.arange(N_SUB)[None, :]]
    count = jnp.minimum(t[:, None] + 1, win[None, :]).astype(jnp.float32)
    y = (window_sum / count[None, :, :, None] - xf).astype(xc.dtype)
    y = jnp.einsum("bsgc,gcd->bsgd", y, w_grp)
    return y.reshape(bn, s, D_G) * scale


def conformer_conv(a, g, w_dw, b_dw, ln_g, ln_b, w_pw):
    h = a * jax.nn.sigmoid(g)
    h = causal_depthwise_conv(h, w_dw) + b_dw
    h = jax.nn.silu(layer_norm(h, ln_g, ln_b))
    return h @ w_pw


def memory_cross_attention(q, mem, w_kv):
    bn, s, _ = q.shape
    k, v = jnp.split(mem @ w_kv, 2, axis=-1)
    q = q.reshape(bn, s, N_SUB, HEAD_DIM)
    k = k.reshape(bn, -1, N_SUB, HEAD_DIM)
    v = v.reshape(bn, -1, N_SUB, HEAD_DIM)
    scores = jnp.einsum("bshd,bmhd->bhsm", q, k).astype(jnp.float32) * (1.0 / math.sqrt(HEAD_DIM))
    p = jax.nn.softmax(scores, axis=-1).astype(v.dtype)
    return jnp.einsum("bhsm,bmhd->bshd", p, v).reshape(bn, s, D_G)


def hybrid_mixer(x, mem, w_in, conv_a_w, sg_ln_g, sg_ln_b, sg_w, sg_b, pool_w, pool_scale,
                 cc_dw_w, cc_dw_b, cc_ln_g, cc_ln_b, cc_pw_w, w_kv, w_out):
    proj = x @ w_in
    xa, ba, ca, u, v, xc, da, dg, q, gate = jnp.split(proj, SPLIT_OFFSETS, axis=-1)
    y_a = short_gated_conv(xa, ba, ca, conv_a_w)
    y_b = spatial_gating(u, v, sg_ln_g, sg_ln_b, sg_w, sg_b)
    y_c = multiscale_pool(xc, pool_w, pool_scale)
    y_d = conformer_conv(da, dg, cc_dw_w, cc_dw_b, cc_ln_g, cc_ln_b, cc_pw_w)
    y_e = memory_cross_attention(q, mem, w_kv)
    h = jnp.concatenate([y_a, y_b, y_c, y_d, y_e], axis=-1) * jax.nn.silu(gate)
    return h @ w_out


def setup_inputs(seed: int = 0) -> dict:
    key = jax.random.key(seed)
    ks = jax.random.split(key, 20)
    L = DEPTH

    def nrm(k, shape, scale):
        return jax.random.normal(k, shape, jnp.float32) * scale

    return {
        "x": nrm(ks[0], (BATCH, SEQ, D_MODEL), 1.0),
        "mem": nrm(ks[1], (BATCH, MEM_LEN, D_MODEL), 1.0),
        "w_in": nrm(ks[2], (L, D_MODEL, D_IN), D_MODEL ** -0.5),
        "conv_a_w": nrm(ks[3], (L, CONV_A, D_G), CONV_A ** -0.5),
        "sg_ln_g": 1.0 + nrm(ks[4], (L, D_G), 0.05),
        "sg_ln_b": nrm(ks[5], (L, D_G), 0.05),
        "sg_w": nrm(ks[6], (L, N_SUB, CHUNK, CHUNK), CHUNK ** -0.5),
        "sg_b": 1.0 + nrm(ks[7], (L, N_SUB, CHUNK), 0.1),
        "pool_w": nrm(ks[8], (L, N_SUB, HEAD_DIM, HEAD_DIM), HEAD_DIM ** -0.5),
        "pool_scale": 1.0 + nrm(ks[9], (L, D_G), 0.1),
        "cc_dw_w": nrm(ks[10], (L, CONV_D, D_G), CONV_D ** -0.5),
        "cc_dw_b": nrm(ks[11], (L, D_G), 0.02),
        "cc_ln_g": 1.0 + nrm(ks[12], (L, D_G), 0.05),
        "cc_ln_b": nrm(ks[13], (L, D_G), 0.05),
        "cc_pw_w": nrm(ks[14], (L, D_G, D_G), D_G ** -0.5),
        "w_kv": nrm(ks[15], (L, D_MODEL, 2 * D_G), D_MODEL ** -0.5),
        "w_out": nrm(ks[16], (L, D_MIX, D_MODEL), D_MIX ** -0.5 * DEEPNORM_BETA),
        "ln_g": 1.0 + nrm(ks[17], (L, D_MODEL), 0.05),
        "ln_b": nrm(ks[18], (L, D_MODEL), 0.05),
    }


def reference(x, mem, w_in, conv_a_w, sg_ln_g, sg_ln_b, sg_w, sg_b, pool_w, pool_scale,
              cc_dw_w, cc_dw_b, cc_ln_g, cc_ln_b, cc_pw_w, w_kv, w_out, ln_g, ln_b):
    for l in range(DEPTH):
        y = hybrid_mixer(x, mem, w_in[l], conv_a_w[l], sg_ln_g[l], sg_ln_b[l], sg_w[l], sg_b[l],
                         pool_w[l], pool_scale[l], cc_dw_w[l], cc_dw_b[l], cc_ln_g[l], cc_ln_b[l],
                         cc_pw_w[l], w_kv[l], w_out[l])
        x = layer_norm(DEEPNORM_ALPHA * x + y, ln_g[l], ln_b[l])
    return x
```

```python
import functools

import jax
import jax.numpy as jnp
from jax import lax
from jax.experimental import pallas as pl
from jax.experimental.pallas import tpu as pltpu

D_MODEL = 1024
D_G = D_MODEL // 4
N_SUB = 4
HEAD_DIM = D_G // N_SUB
D_MIX = 5 * D_G
D_IN = 9 * D_G + D_MIX
CHUNK = 128
CONV_A = 3
CONV_D = 31
POOL_WINDOWS = (2, 4, 8, 16)
LN_EPS = 1e-5

OFF_A = 0
OFF_B = 3 * D_G
OFF_C = 5 * D_G
OFF_D = 6 * D_G
OFF_E = 8 * D_G
OFF_GATE = 9 * D_G

HALO_A = 8
HALO_C = 16
HALO_D = 32
CONV_ROWS = 32

SEQ_TILE = 256
VMEM_LIMIT_BYTES = 48 * 1024 * 1024

F32 = jnp.float32
BF16 = jnp.bfloat16


def _dot(a, b):
    return jnp.dot(a, b, preferred_element_type=F32)


def _layer_norm(x, g, b):
    mu = jnp.mean(x, axis=-1, keepdims=True)
    xc = x - mu
    var = jnp.mean(xc * xc, axis=-1, keepdims=True)
    return xc * lax.rsqrt(var + LN_EPS) * g + b


def _kv_kernel(mem_ref, wkt_ref, wv_ref, kbd_ref, vbd_ref):
    m = mem_ref[0].astype(BF16)
    mem_len = m.shape[0]
    kt = lax.dot_general(wkt_ref[...], m, (((1,), (1,)), ((), ())),
                         preferred_element_type=F32)
    kt = kt * (1.0 / (HEAD_DIM ** 0.5))
    v = _dot(m, wv_ref[...])
    row_head = lax.broadcasted_iota(jnp.int32, kt.shape, 0) // HEAD_DIM
    col_head = lax.broadcasted_iota(jnp.int32, v.shape, 1) // HEAD_DIM
    for h in range(N_SUB):
        kbd_ref[0, :, h * mem_len:(h + 1) * mem_len] = jnp.where(row_head == h, kt, 0.0).astype(BF16)
        vbd_ref[0, h * mem_len:(h + 1) * mem_len, :] = jnp.where(col_head == h, v, 0.0).astype(BF16)


def _layer_kernel(x_ref, kbd_ref, vbd_ref, w_in_ref, conv_a_ref, sg_g_ref, sg_b_ref, sg_w_ref,
                  sg_bias_ref, pool_w_ref, pool_scale_ref, dw_w_ref, dw_b_ref, cc_g_ref, cc_b_ref,
                  pw_ref, w_out_ref, ln_g_ref, ln_b_ref, o_ref, cx_scr, xc_scr, h_scr, *, tile, alpha):
    s = pl.program_id(1)

    @pl.when(s == 0)
    def _():
        cx_scr[0:HALO_A, :] = jnp.zeros((HALO_A, D_G), F32)
        xc_scr[0:HALO_C, :] = jnp.zeros((HALO_C, D_G), F32)
        h_scr[0:HALO_D, :] = jnp.zeros((HALO_D, D_G), F32)

    x = x_ref[0]
    xb = x.astype(BF16)

    def proj(lo, width):
        return _dot(xb, w_in_ref[:, lo:lo + width])

    def gated(y, lo):
        return (y * jax.nn.silu(proj(OFF_GATE + lo, D_G))).astype(BF16)

    pa = proj(OFF_A, 3 * D_G)
    xa, ba, ca = pa[:, :D_G], pa[:, D_G:2 * D_G], pa[:, 2 * D_G:]
    cx = ca * xa
    cx_scr[HALO_A:HALO_A + tile, :] = cx
    conv_a = conv_a_ref[2:3, :] * cx
    for k in range(CONV_A - 1):
        off = HALO_A - (CONV_A - 1) + k
        conv_a = conv_a + conv_a_ref[k:k + 1, :] * cx_scr[off:off + tile, :]
    h_a = gated(ba * conv_a, 0)
    cx_scr[0:HALO_A, :] = cx_scr[tile:tile + HALO_A, :]

    pb = proj(OFF_B, 2 * D_G)
    u = jax.nn.gelu(pb[:, :D_G])
    v = _layer_norm(jax.nn.gelu(pb[:, D_G:]), sg_g_ref[...], sg_b_ref[...])
    vb = v.astype(BF16)
    wt = lax.broadcasted_iota(jnp.int32, (CHUNK, N_SUB * CHUNK), 0)
    ws = lax.broadcasted_iota(jnp.int32, (CHUNK, N_SUB * CHUNK), 1) % CHUNK
    w_mix = jnp.where(ws <= wt, sg_w_ref[...], 0.0).astype(BF16)
    lane_head = lax.broadcasted_iota(jnp.int32, (CHUNK, D_G), 1) // HEAD_DIM
    mixed = []
    for c in range(tile // CHUNK):
        vc = vb[c * CHUNK:(c + 1) * CHUNK, :]
        rhs = jnp.concatenate(
            [jnp.where(lane_head == h, vc, jnp.zeros_like(vc)) for h in range(N_SUB)], axis=0)
        mixed.append(_dot(w_mix, rhs) + sg_bias_ref[...])
    h_b = gated(u * jnp.concatenate(mixed, axis=0), D_G)

    xc = proj(OFF_C, D_G)
    xc_scr[HALO_C:HALO_C + tile, :] = xc

    def back(j, lo):
        return xc_scr[HALO_C - j:HALO_C - j + tile, lo:lo + 128]

    lane = lax.broadcasted_iota(jnp.int32, (tile, 128), 1)
    s2 = xc[:, :128] + back(1, 0)
    s4 = s2 + (back(2, 0) + back(3, 0))
    s8 = xc[:, 128:]
    for j in range(1, 8):
        s8 = s8 + back(j, 128)
    s16 = s8
    for j in range(8, 16):
        s16 = s16 + back(j, 128)
    win_sum = jnp.concatenate([jnp.where(lane < HEAD_DIM, s2, s4),
                               jnp.where(lane < HEAD_DIM, s8, s16)], axis=1)
    t_abs = s * tile + lax.broadcasted_iota(jnp.int32, (tile, D_G), 0)
    lane_g = lax.broadcasted_iota(jnp.int32, (tile, D_G), 1) // HEAD_DIM
    window = jnp.left_shift(2, lane_g)
    count = jnp.minimum(t_abs + 1, window).astype(F32)
    pooled = (win_sum / count - xc).astype(BF16)
    h_c = gated(_dot(pooled, pool_w_ref[...]) * pool_scale_ref[...], 2 * D_G)
    xc_scr[0:HALO_C, :] = xc_scr[tile:tile + HALO_C, :]

    pd = proj(OFF_D, 2 * D_G)
    h_scr[HALO_D:HALO_D + tile, :] = pd[:, :D_G] * jax.nn.sigmoid(pd[:, D_G:])
    conv_d = []
    for r in range(tile // CONV_ROWS):
        acc = jnp.broadcast_to(dw_b_ref[...], (CONV_ROWS, D_G))
        for k in range(CONV_D):
            off = HALO_D - (CONV_D - 1) + k + r * CONV_ROWS
            acc = acc + dw_w_ref[k:k + 1, :] * h_scr[off:off + CONV_ROWS, :]
        conv_d.append(acc)
    hn = jax.nn.silu(_layer_norm(jnp.concatenate(conv_d, axis=0), cc_g_ref[...], cc_b_ref[...]))
    h_d = gated(_dot(hn.astype(BF16), pw_ref[...]), 3 * D_G)
    h_scr[0:HALO_D, :] = h_scr[tile:tile + HALO_D, :]

    q = proj(OFF_E, D_G).astype(BF16)
    scores = _dot(q, kbd_ref[0])
    mem_len = scores.shape[1] // N_SUB
    probs = []
    for h in range(N_SUB):
        sh = scores[:, h * mem_len:(h + 1) * mem_len]
        e = jnp.exp(sh - jnp.max(sh, axis=-1, keepdims=True))
        inv = 1.0 / jnp.sum(e, axis=-1, keepdims=True)
        probs.append((e * inv).astype(BF16))
    h_e = gated(_dot(jnp.concatenate(probs, axis=1), vbd_ref[0]), 4 * D_G)

    hcat = jnp.concatenate([h_a, h_b, h_c, h_d, h_e], axis=1)
    y = _dot(hcat, w_out_ref[...])
    o_ref[0] = _layer_norm(alpha * x + y, ln_g_ref[...], ln_b_ref[...])


def _const_spec(shape):
    return pl.BlockSpec(shape, lambda b, s: (0,) * len(shape))


def _kv_call(mem, wkt, wv):
    batch, mem_len, _ = mem.shape
    return pl.pallas_call(
        _kv_kernel,
        grid=(batch,),
        in_specs=[
            pl.BlockSpec((1, mem_len, D_MODEL), lambda b: (b, 0, 0)),
            pl.BlockSpec((D_G, D_MODEL), lambda b: (0, 0)),
            pl.BlockSpec((D_MODEL, D_G), lambda b: (0, 0)),
        ],
        out_specs=[
            pl.BlockSpec((1, D_G, N_SUB * mem_len), lambda b: (b, 0, 0)),
            pl.BlockSpec((1, N_SUB * mem_len, D_G), lambda b: (b, 0, 0)),
        ],
        out_shape=[
            jax.ShapeDtypeStruct((batch, D_G, N_SUB * mem_len), BF16),
            jax.ShapeDtypeStruct((batch, N_SUB * mem_len, D_G), BF16),
        ],
        compiler_params=pltpu.CompilerParams(dimension_semantics=("arbitrary",)),
        name="mem_kv",
    )(mem, wkt, wv)


def _layer_call(x, kbd, vbd, params, alpha):
    batch, seq, _ = x.shape
    tile = SEQ_TILE
    assert seq % tile == 0 and tile % CHUNK == 0 and tile % CONV_ROWS == 0
    kv_width = kbd.shape[2]
    in_specs = [
        pl.BlockSpec((1, tile, D_MODEL), lambda b, s: (b, s, 0)),
        pl.BlockSpec((1, D_G, kv_width), lambda b, s: (b, 0, 0)),
        pl.BlockSpec((1, kv_width, D_G), lambda b, s: (b, 0, 0)),
    ] + [_const_spec(p.shape) for p in params]
    return pl.pallas_call(
        functools.partial(_layer_kernel, tile=tile, alpha=alpha),
        grid=(batch, seq // tile),
        in_specs=in_specs,
        out_specs=pl.BlockSpec((1, tile, D_MODEL), lambda b, s: (b, s, 0)),
        out_shape=jax.ShapeDtypeStruct(x.shape, x.dtype),
        scratch_shapes=[
            pltpu.VMEM((HALO_A + tile, D_G), F32),
            pltpu.VMEM((HALO_C + tile, D_G), F32),
            pltpu.VMEM((HALO_D + tile, D_G), F32),
        ],
        compiler_params=pltpu.CompilerParams(
            dimension_semantics=("arbitrary", "arbitrary"),
            vmem_limit_bytes=VMEM_LIMIT_BYTES),
        name="mixer_layer",
    )(x, kbd, vbd, *params)


def kernel(x, mem, w_in, conv_a_w, sg_ln_g, sg_ln_b, sg_w, sg_b, pool_w, pool_scale, cc_dw_w, cc_dw_b, cc_ln_g, cc_ln_b, cc_pw_w, w_kv, w_out, ln_g, ln_b):
    depth = w_in.shape[0]
    alpha = (2.0 * depth) ** 0.25
    row = lambda a: a.reshape(1, -1)
    for l in range(depth):
        wkt = w_kv[l][:, :D_G].T.astype(BF16)
        wv = w_kv[l][:, D_G:].astype(BF16)
        sg_w_cat = sg_w[l].transpose(1, 0, 2).reshape(CHUNK, N_SUB * CHUNK)
        sg_bias = jnp.repeat(sg_b[l].T, HEAD_DIM, axis=1)
        pool_bd = jax.scipy.linalg.block_diag(*[pool_w[l, g] for g in range(N_SUB)]).astype(BF16)
        params = (
            w_in[l].astype(BF16), conv_a_w[l], row(sg_ln_g[l]), row(sg_ln_b[l]), sg_w_cat, sg_bias,
            pool_bd, row(pool_scale[l]), cc_dw_w[l], row(cc_dw_b[l]), row(cc_ln_g[l]), row(cc_ln_b[l]),
            cc_pw_w[l].astype(BF16), w_out[l].astype(BF16), row(ln_g[l]), row(ln_b[l]),
        )
        kbd, vbd = _kv_call(mem, wkt, wv)
        x = _layer_call(x, kbd, vbd, params, alpha)
    return x
```

```python
import functools

import jax
import jax.numpy as jnp
from jax import lax
from jax.experimental import pallas as pl
from jax.experimental.pallas import tpu as pltpu

D_MODEL = 1024
D_G = D_MODEL // 4
N_SUB = 4
HEAD_DIM = D_G // N_SUB
D_MIX = 5 * D_G
D_IN = 9 * D_G + D_MIX
CHUNK = 128
CONV_A = 3
CONV_D = 31
POOL_WINDOWS = (2, 4, 8, 16)
LN_EPS = 1e-5

OFF_A = 0
OFF_B = 3 * D_G
OFF_C = 5 * D_G
OFF_D = 6 * D_G
OFF_E = 8 * D_G
OFF_GATE = 9 * D_G

HALO_A = 8
HALO_C = 16
HALO_D = 32
CONV_ROWS = 32

SEQ_TILE = 256
VMEM_LIMIT_BYTES = 48 * 1024 * 1024

F32 = jnp.float32
BF16 = jnp.bfloat16


def _dot(a, b):
    return jnp.dot(a, b, preferred_element_type=F32)


def _layer_norm(x, g, b):
    mu = jnp.mean(x, axis=-1, keepdims=True)
    xc = x - mu
    var = jnp.mean(xc * xc, axis=-1, keepdims=True)
    return xc * lax.rsqrt(var + LN_EPS) * g + b


def _kv_kernel(mem_ref, wkt_ref, wv_ref, kbd_ref, vbd_ref):
    m = mem_ref[0].astype(BF16)
    mem_len = m.shape[0]
    kt = lax.dot_general(wkt_ref[...], m, (((1,), (1,)), ((), ())),
                         preferred_element_type=F32)
    kt = kt * (1.0 / (HEAD_DIM ** 0.5))
    v = _dot(m, wv_ref[...])
    row_head = lax.broadcasted_iota(jnp.int32, kt.shape, 0) // HEAD_DIM
    col_head = lax.broadcasted_iota(jnp.int32, v.shape, 1) // HEAD_DIM
    for h in range(N_SUB):
        kbd_ref[0, :, h * mem_len:(h + 1) * mem_len] = jnp.where(row_head == h, kt, 0.0).astype(BF16)
        vbd_ref[0, h * mem_len:(h + 1) * mem_len, :] = jnp.where(col_head == h, v, 0.0).astype(BF16)


def _layer_kernel(x_ref, kbd_ref, vbd_ref, w_in_ref, conv_a_ref, sg_g_ref, sg_b_ref, sg_w_ref,
                  sg_bias_ref, pool_w_ref, pool_scale_ref, dw_w_ref, dw_b_ref, cc_g_ref, cc_b_ref,
                  pw_ref, w_out_ref, ln_g_ref, ln_b_ref, o_ref, cx_scr, xc_scr, h_scr, *, tile, alpha):
    s = pl.program_id(1)

    @pl.when(s == 0)
    def _():
        cx_scr[0:HALO_A, :] = jnp.zeros((HALO_A, D_G), F32)
        xc_scr[0:HALO_C, :] = jnp.zeros((HALO_C, D_G), F32)
        h_scr[0:HALO_D, :] = jnp.zeros((HALO_D, D_G), F32)

    x = x_ref[0]
    xb = x.astype(BF16)

    def proj(lo, width):
        return _dot(xb, w_in_ref[:, lo:lo + width])

    def gated(y, lo):
        return (y * jax.nn.silu(proj(OFF_GATE + lo, D_G))).astype(BF16)

    pa = proj(OFF_A, 3 * D_G)
    xa, ba, ca = pa[:, :D_G], pa[:, D_G:2 * D_G], pa[:, 2 * D_G:]
    cx = ca * xa
    cx_scr[HALO_A:HALO_A + tile, :] = cx
    conv_a = conv_a_ref[2:3, :] * cx
    for k in range(CONV_A - 1):
        off = HALO_A - (CONV_A - 1) + k
        conv_a = conv_a + conv_a_ref[k:k + 1, :] * cx_scr[off:off + tile, :]
    h_a = gated(ba * conv_a, 0)
    cx_scr[0:HALO_A, :] = cx_scr[tile:tile + HALO_A, :]

    pb = proj(OFF_B, 2 * D_G)
    u = jax.nn.gelu(pb[:, :D_G])
    v = _layer_norm(jax.nn.gelu(pb[:, D_G:]), sg_g_ref[...], sg_b_ref[...])
    vb = v.astype(BF16)
    wt = lax.broadcasted_iota(jnp.int32, (CHUNK, N_SUB * CHUNK), 0)
    ws = lax.broadcasted_iota(jnp.int32, (CHUNK, N_SUB * CHUNK), 1) % CHUNK
    w_mix = jnp.where(ws <= wt, sg_w_ref[...], 0.0).astype(BF16)
    lane_head = lax.broadcasted_iota(jnp.int32, (CHUNK, D_G), 1) // HEAD_DIM
    mixed = []
    for c in range(tile // CHUNK):
        vc = vb[c * CHUNK:(c + 1) * CHUNK, :]
        rhs = jnp.concatenate(
            [jnp.where(lane_head == h, vc, jnp.zeros_like(vc)) for h in range(N_SUB)], axis=0)
        mixed.append(_dot(w_mix, rhs) + sg_bias_ref[...])
    h_b = gated(u * jnp.concatenate(mixed, axis=0), D_G)

    xc = proj(OFF_C, D_G)
    xc_scr[HALO_C:HALO_C + tile, :] = xc

    xe = xc_scr[...]

    def doubled(v, steps):
        for shift in steps:
            v = v + pltpu.roll(v, shift, axis=0)
        return v

    s2 = doubled(xe[:, :128], (1,))
    s4 = doubled(s2, (2,))
    s8 = doubled(xe[:, 128:], (1, 2, 4))
    s16 = doubled(s8, (8,))
    lane = lax.broadcasted_iota(jnp.int32, s2.shape, 1)
    win_sum = jnp.concatenate([jnp.where(lane < HEAD_DIM, s2, s4),
                               jnp.where(lane < HEAD_DIM, s8, s16)], axis=1)[HALO_C:, :]
    t_abs = s * tile + lax.broadcasted_iota(jnp.int32, (tile, D_G), 0)
    lane_g = lax.broadcasted_iota(jnp.int32, (tile, D_G), 1) // HEAD_DIM
    window = jnp.left_shift(2, lane_g)
    count = jnp.minimum(t_abs + 1, window).astype(F32)
    pooled = (win_sum / count - xc).astype(BF16)
    h_c = gated(_dot(pooled, pool_w_ref[...]) * pool_scale_ref[...], 2 * D_G)
    xc_scr[0:HALO_C, :] = xc_scr[tile:tile + HALO_C, :]

    pd = proj(OFF_D, 2 * D_G)
    h_scr[HALO_D:HALO_D + tile, :] = pd[:, :D_G] * jax.nn.sigmoid(pd[:, D_G:])
    conv_d = []
    lead = HALO_D - (CONV_D - 1)
    for c in range(tile // CONV_ROWS):
        base = c * CONV_ROWS
        acc = jnp.broadcast_to(dw_b_ref[...], (CONV_ROWS, D_G))
        for r in range(8):
            rows = CONV_ROWS + (8 if r else 0)
            part = None
            for j in range(r, lead + CONV_D, 8):
                if j < lead:
                    continue
                term = dw_w_ref[j - lead:j - lead + 1, :] * h_scr[base + j - r:base + j - r + rows, :]
                part = term if part is None else part + term
            acc = acc + part[r:r + CONV_ROWS, :]
        conv_d.append(acc)
    hn = jax.nn.silu(_layer_norm(jnp.concatenate(conv_d, axis=0), cc_g_ref[...], cc_b_ref[...]))
    h_d = gated(_dot(hn.astype(BF16), pw_ref[...]), 3 * D_G)
    h_scr[0:HALO_D, :] = h_scr[tile:tile + HALO_D, :]

    q = proj(OFF_E, D_G).astype(BF16)
    scores = _dot(q, kbd_ref[0])
    mem_len = scores.shape[1] // N_SUB
    probs = []
    for h in range(N_SUB):
        sh = scores[:, h * mem_len:(h + 1) * mem_len]
        e = jnp.exp(sh - jnp.max(sh, axis=-1, keepdims=True))
        inv = 1.0 / jnp.sum(e, axis=-1, keepdims=True)
        probs.append((e * inv).astype(BF16))
    h_e = gated(_dot(jnp.concatenate(probs, axis=1), vbd_ref[0]), 4 * D_G)

    hcat = jnp.concatenate([h_a, h_b, h_c, h_d, h_e], axis=1)
    y = _dot(hcat, w_out_ref[...])
    o_ref[0] = _layer_norm(alpha * x + y, ln_g_ref[...], ln_b_ref[...])


def _const_spec(shape):
    return pl.BlockSpec(shape, lambda b, s: (0,) * len(shape))


def _kv_call(mem, wkt, wv):
    batch, mem_len, _ = mem.shape
    return pl.pallas_call(
        _kv_kernel,
        grid=(batch,),
        in_specs=[
            pl.BlockSpec((1, mem_len, D_MODEL), lambda b: (b, 0, 0)),
            pl.BlockSpec((D_G, D_MODEL), lambda b: (0, 0)),
            pl.BlockSpec((D_MODEL, D_G), lambda b: (0, 0)),
        ],
        out_specs=[
            pl.BlockSpec((1, D_G, N_SUB * mem_len), lambda b: (b, 0, 0)),
            pl.BlockSpec((1, N_SUB * mem_len, D_G), lambda b: (b, 0, 0)),
        ],
        out_shape=[
            jax.ShapeDtypeStruct((batch, D_G, N_SUB * mem_len), BF16),
            jax.ShapeDtypeStruct((batch, N_SUB * mem_len, D_G), BF16),
        ],
        compiler_params=pltpu.CompilerParams(dimension_semantics=("arbitrary",)),
        name="mem_kv",
    )(mem, wkt, wv)


def _layer_call(x, kbd, vbd, params, alpha):
    batch, seq, _ = x.shape
    tile = SEQ_TILE
    assert seq % tile == 0 and tile % CHUNK == 0 and tile % CONV_ROWS == 0
    kv_width = kbd.shape[2]
    in_specs = [
        pl.BlockSpec((1, tile, D_MODEL), lambda b, s: (b, s, 0)),
        pl.BlockSpec((1, D_G, kv_width), lambda b, s: (b, 0, 0)),
        pl.BlockSpec((1, kv_width, D_G), lambda b, s: (b, 0, 0)),
    ] + [_const_spec(p.shape) for p in params]
    return pl.pallas_call(
        functools.partial(_layer_kernel, tile=tile, alpha=alpha),
        grid=(batch, seq // tile),
        in_specs=in_specs,
        out_specs=pl.BlockSpec((1, tile, D_MODEL), lambda b, s: (b, s, 0)),
        out_shape=jax.ShapeDtypeStruct(x.shape, x.dtype),
        scratch_shapes=[
            pltpu.VMEM((HALO_A + tile, D_G), F32),
            pltpu.VMEM((HALO_C + tile, D_G), F32),
            pltpu.VMEM((HALO_D + tile, D_G), F32),
        ],
        compiler_params=pltpu.CompilerParams(
            dimension_semantics=("arbitrary", "arbitrary"),
            vmem_limit_bytes=VMEM_LIMIT_BYTES),
        name="mixer_layer",
    )(x, kbd, vbd, *params)


def kernel(x, mem, w_in, conv_a_w, sg_ln_g, sg_ln_b, sg_w, sg_b, pool_w, pool_scale, cc_dw_w, cc_dw_b, cc_ln_g, cc_ln_b, cc_pw_w, w_kv, w_out, ln_g, ln_b):
    depth = w_in.shape[0]
    alpha = (2.0 * depth) ** 0.25
    row = lambda a: a.reshape(1, -1)
    for l in range(depth):
        wkt = w_kv[l][:, :D_G].T.astype(BF16)
        wv = w_kv[l][:, D_G:].astype(BF16)
        sg_w_cat = sg_w[l].transpose(1, 0, 2).reshape(CHUNK, N_SUB * CHUNK)
        sg_bias = jnp.repeat(sg_b[l].T, HEAD_DIM, axis=1)
        pool_bd = jax.scipy.linalg.block_diag(*[pool_w[l, g] for g in range(N_SUB)]).astype(BF16)
        params = (
            w_in[l].astype(BF16), conv_a_w[l], row(sg_ln_g[l]), row(sg_ln_b[l]), sg_w_cat, sg_bias,
            pool_bd, row(pool_scale[l]), cc_dw_w[l], row(cc_dw_b[l]), row(cc_ln_g[l]), row(cc_ln_b[l]),
            cc_pw_w[l].astype(BF16), w_out[l].astype(BF16), row(ln_g[l]), row(ln_b[l]),
        )
        kbd, vbd = _kv_call(mem, wkt, wv)
        x = _layer_call(x, kbd, vbd, params, alpha)
    return x
```

```python
import functools

import jax
import jax.numpy as jnp
from jax import lax
from jax.experimental import pallas as pl
from jax.experimental.pallas import tpu as pltpu

D_MODEL = 1024
D_G = D_MODEL // 4
N_SUB = 4
HEAD_DIM = D_G // N_SUB
D_MIX = 5 * D_G
D_IN = 9 * D_G + D_MIX
CHUNK = 128
CONV_A = 3
CONV_D = 31
POOL_WINDOWS = (2, 4, 8, 16)
LN_EPS = 1e-5

OFF_A = 0
OFF_B = 3 * D_G
OFF_C = 5 * D_G
OFF_D = 6 * D_G
OFF_E = 8 * D_G
OFF_GATE = 9 * D_G

HALO_A = 8
HALO_C = 16
HALO_D = 32
CONV_ROWS = 32

PROJ_GROUPS = ((OFF_A, 3 * D_G), (OFF_B, 2 * D_G), (OFF_C, D_G), (OFF_D, 2 * D_G), (OFF_E, D_G)) + tuple(
    (OFF_GATE + g * D_G, D_G) for g in range(5))

SEQ_TILE = 256
VMEM_LIMIT_BYTES = 48 * 1024 * 1024

F32 = jnp.float32
BF16 = jnp.bfloat16


def _dot(a, b):
    return jnp.dot(a, b, preferred_element_type=F32)


def _layer_norm(x, g, b):
    mu = jnp.mean(x, axis=-1, keepdims=True)
    xc = x - mu
    var = jnp.mean(xc * xc, axis=-1, keepdims=True)
    return xc * lax.rsqrt(var + LN_EPS) * g + b


def _kv_kernel(mem_ref, wkt_ref, wv_ref, kbd_ref, vbd_ref):
    m = mem_ref[0].astype(BF16)
    mem_len = m.shape[0]
    kt = lax.dot_general(wkt_ref[...], m, (((1,), (1,)), ((), ())),
                         preferred_element_type=F32)
    kt = kt * (1.0 / (HEAD_DIM ** 0.5))
    v = _dot(m, wv_ref[...])
    row_head = lax.broadcasted_iota(jnp.int32, kt.shape, 0) // HEAD_DIM
    col_head = lax.broadcasted_iota(jnp.int32, v.shape, 1) // HEAD_DIM
    for h in range(N_SUB):
        kbd_ref[0, :, h * mem_len:(h + 1) * mem_len] = jnp.where(row_head == h, kt, 0.0).astype(BF16)
        vbd_ref[0, h * mem_len:(h + 1) * mem_len, :] = jnp.where(col_head == h, v, 0.0).astype(BF16)


def _layer_kernel(x_ref, xn_ref, kbd_ref, vbd_ref, w_in_ref, conv_a_ref, sg_g_ref, sg_b_ref, sg_w_ref,
                  sg_bias_ref, pool_w_ref, pool_scale_ref, dw_w_ref, dw_b_ref, cc_g_ref, cc_b_ref,
                  pw_ref, w_out_ref, ln_g_ref, ln_b_ref, o_ref, xb_scr, cx_scr, xc_scr, h_scr, *proj_scrs,
                  tile, alpha):
    b = pl.program_id(0)
    s = pl.program_id(1)

    proj_of = dict(zip(PROJ_GROUPS, proj_scrs))

    @pl.when((b == 0) & (s == 0))
    def _():
        xb0 = x_ref[0].astype(BF16)
        for (lo, width), scr in proj_of.items():
            scr[...] = _dot(xb0, w_in_ref[:, lo:lo + width])

    @pl.when(s == 0)
    def _():
        cx_scr[0:HALO_A, :] = jnp.zeros((HALO_A, D_G), F32)
        xc_scr[0:HALO_C, :] = jnp.zeros((HALO_C, D_G), F32)
        h_scr[0:HALO_D, :] = jnp.zeros((HALO_D, D_G), F32)

    xb_scr[...] = xn_ref[0].astype(BF16)

    def take(lo, width):
        scr = proj_of[(lo, width)]
        cols = scr[...]
        scr[...] = _dot(xb_scr[...], w_in_ref[:, lo:lo + width])
        return cols

    def gated(y, lo):
        return (y * jax.nn.silu(take(OFF_GATE + lo, D_G))).astype(BF16)

    pd = take(OFF_D, 2 * D_G)
    h_scr[HALO_D:HALO_D + tile, :] = pd[:, :D_G] * jax.nn.sigmoid(pd[:, D_G:])
    conv_d = []
    lead = HALO_D - (CONV_D - 1)
    for c in range(tile // CONV_ROWS):
        base = c * CONV_ROWS
        acc = jnp.broadcast_to(dw_b_ref[...], (CONV_ROWS, D_G))
        for r in range(8):
            rows = CONV_ROWS + (8 if r else 0)
            part = None
            for j in range(r, lead + CONV_D, 8):
                if j < lead:
                    continue
                term = dw_w_ref[j - lead:j - lead + 1, :] * h_scr[base + j - r:base + j - r + rows, :]
                part = term if part is None else part + term
            acc = acc + part[r:r + CONV_ROWS, :]
        conv_d.append(acc)
    hn = jax.nn.silu(_layer_norm(jnp.concatenate(conv_d, axis=0), cc_g_ref[...], cc_b_ref[...]))
    h_d = gated(_dot(hn.astype(BF16), pw_ref[...]), 3 * D_G)
    h_scr[0:HALO_D, :] = h_scr[tile:tile + HALO_D, :]

    pa = take(OFF_A, 3 * D_G)
    xa, ba, ca = pa[:, :D_G], pa[:, D_G:2 * D_G], pa[:, 2 * D_G:]
    cx = ca * xa
    cx_scr[HALO_A:HALO_A + tile, :] = cx
    conv_a = conv_a_ref[2:3, :] * cx
    for k in range(CONV_A - 1):
        off = HALO_A - (CONV_A - 1) + k
        conv_a = conv_a + conv_a_ref[k:k + 1, :] * cx_scr[off:off + tile, :]
    h_a = gated(ba * conv_a, 0)
    cx_scr[0:HALO_A, :] = cx_scr[tile:tile + HALO_A, :]

    q = take(OFF_E, D_G).astype(BF16)
    scores = _dot(q, kbd_ref[0])
    mem_len = scores.shape[1] // N_SUB
    probs = []
    for h in range(N_SUB):
        sh = scores[:, h * mem_len:(h + 1) * mem_len]
        e = jnp.exp(sh - jnp.max(sh, axis=-1, keepdims=True))
        inv = 1.0 / jnp.sum(e, axis=-1, keepdims=True)
        probs.append((e * inv).astype(BF16))
    h_e = gated(_dot(jnp.concatenate(probs, axis=1), vbd_ref[0]), 4 * D_G)

    pb = take(OFF_B, 2 * D_G)
    u = jax.nn.gelu(pb[:, :D_G])
    v = _layer_norm(jax.nn.gelu(pb[:, D_G:]), sg_g_ref[...], sg_b_ref[...])
    vb = v.astype(BF16)
    wt = lax.broadcasted_iota(jnp.int32, (CHUNK, N_SUB * CHUNK), 0)
    ws = lax.broadcasted_iota(jnp.int32, (CHUNK, N_SUB * CHUNK), 1) % CHUNK
    w_mix = jnp.where(ws <= wt, sg_w_ref[...], 0.0).astype(BF16)
    lane_head = lax.broadcasted_iota(jnp.int32, (CHUNK, D_G), 1) // HEAD_DIM
    mixed = []
    for c in range(tile // CHUNK):
        vc = vb[c * CHUNK:(c + 1) * CHUNK, :]
        rhs = jnp.concatenate(
            [jnp.where(lane_head == h, vc, jnp.zeros_like(vc)) for h in range(N_SUB)], axis=0)
        mixed.append(_dot(w_mix, rhs) + sg_bias_ref[...])
    h_b = gated(u * jnp.concatenate(mixed, axis=0), D_G)

    xc = take(OFF_C, D_G)
    xc_scr[HALO_C:HALO_C + tile, :] = xc

    xe = xc_scr[...]

    def doubled(v, steps):
        for shift in steps:
            v = v + pltpu.roll(v, shift, axis=0)
        return v

    s2 = doubled(xe[:, :128], (1,))
    s4 = doubled(s2, (2,))
    s8 = doubled(xe[:, 128:], (1, 2, 4))
    s16 = doubled(s8, (8,))
    lane = lax.broadcasted_iota(jnp.int32, s2.shape, 1)
    win_sum = jnp.concatenate([jnp.where(lane < HEAD_DIM, s2, s4),
                               jnp.where(lane < HEAD_DIM, s8, s16)], axis=1)[HALO_C:, :]
    t_abs = s * tile + lax.broadcasted_iota(jnp.int32, (tile, D_G), 0)
    lane_g = lax.broadcasted_iota(jnp.int32, (tile, D_G), 1) // HEAD_DIM
    window = jnp.left_shift(2, lane_g)
    count = jnp.minimum(t_abs + 1, window).astype(F32)
    pooled = (win_sum / count - xc).astype(BF16)
    h_c = gated(_dot(pooled, pool_w_ref[...]) * pool_scale_ref[...], 2 * D_G)
    xc_scr[0:HALO_C, :] = xc_scr[tile:tile + HALO_C, :]

    hcat = jnp.concatenate([h_a, h_b, h_c, h_d, h_e], axis=1)
    y = _dot(hcat, w_out_ref[...])
    o_ref[0] = _layer_norm(alpha * x_ref[0] + y, ln_g_ref[...], ln_b_ref[...])


def _const_spec(shape):
    return pl.BlockSpec(shape, lambda b, s: (0,) * len(shape))


def _kv_call(mem, wkt, wv):
    batch, mem_len, _ = mem.shape
    return pl.pallas_call(
        _kv_kernel,
        grid=(batch,),
        in_specs=[
            pl.BlockSpec((1, mem_len, D_MODEL), lambda b: (b, 0, 0)),
            pl.BlockSpec((D_G, D_MODEL), lambda b: (0, 0)),
            pl.BlockSpec((D_MODEL, D_G), lambda b: (0, 0)),
        ],
        out_specs=[
            pl.BlockSpec((1, D_G, N_SUB * mem_len), lambda b: (b, 0, 0)),
            pl.BlockSpec((1, N_SUB * mem_len, D_G), lambda b: (b, 0, 0)),
        ],
        out_shape=[
            jax.ShapeDtypeStruct((batch, D_G, N_SUB * mem_len), BF16),
            jax.ShapeDtypeStruct((batch, N_SUB * mem_len, D_G), BF16),
        ],
        compiler_params=pltpu.CompilerParams(dimension_semantics=("arbitrary",)),
        name="mem_kv",
    )(mem, wkt, wv)


def _layer_call(x, kbd, vbd, params, alpha):
    batch, seq, _ = x.shape
    tile = SEQ_TILE
    assert seq % tile == 0 and tile % CHUNK == 0 and tile % CONV_ROWS == 0
    n_seq = seq // tile
    kv_width = kbd.shape[2]

    def next_tile(b, s):
        wrap = s + 1 == n_seq
        return (jnp.where(wrap, jnp.minimum(b + 1, batch - 1), b), jnp.where(wrap, 0, s + 1), 0)

    in_specs = [
        pl.BlockSpec((1, tile, D_MODEL), lambda b, s: (b, s, 0)),
        pl.BlockSpec((1, tile, D_MODEL), next_tile),
        pl.BlockSpec((1, D_G, kv_width), lambda b, s: (b, 0, 0)),
        pl.BlockSpec((1, kv_width, D_G), lambda b, s: (b, 0, 0)),
    ] + [_const_spec(p.shape) for p in params]
    return pl.pallas_call(
        functools.partial(_layer_kernel, tile=tile, alpha=alpha),
        grid=(batch, n_seq),
        in_specs=in_specs,
        out_specs=pl.BlockSpec((1, tile, D_MODEL), lambda b, s: (b, s, 0)),
        out_shape=jax.ShapeDtypeStruct(x.shape, x.dtype),
        scratch_shapes=[
            pltpu.VMEM((tile, D_MODEL), BF16),
            pltpu.VMEM((HALO_A + tile, D_G), F32),
            pltpu.VMEM((HALO_C + tile, D_G), F32),
            pltpu.VMEM((HALO_D + tile, D_G), F32),
        ] + [pltpu.VMEM((tile, width), F32) for _, width in PROJ_GROUPS],
        compiler_params=pltpu.CompilerParams(
            dimension_semantics=("arbitrary", "arbitrary"),
            vmem_limit_bytes=VMEM_LIMIT_BYTES),
        name="mixer_layer",
    )(x, x, kbd, vbd, *params)


def kernel(x, mem, w_in, conv_a_w, sg_ln_g, sg_ln_b, sg_w, sg_b, pool_w, pool_scale, cc_dw_w, cc_dw_b, cc_ln_g, cc_ln_b, cc_pw_w, w_kv, w_out, ln_g, ln_b):
    depth = w_in.shape[0]
    alpha = (2.0 * depth) ** 0.25
    row = lambda a: a.reshape(1, -1)
    for l in range(depth):
        wkt = w_kv[l][:, :D_G].T.astype(BF16)
        wv = w_kv[l][:, D_G:].astype(BF16)
        sg_w_cat = sg_w[l].transpose(1, 0, 2).reshape(CHUNK, N_SUB * CHUNK)
        sg_bias = jnp.repeat(sg_b[l].T, HEAD_DIM, axis=1)
        pool_bd = jax.scipy.linalg.block_diag(*[pool_w[l, g] for g in range(N_SUB)]).astype(BF16)
        params = (
            w_in[l].astype(BF16), conv_a_w[l], row(sg_ln_g[l]), row(sg_ln_b[l]), sg_w_cat, sg_bias,
            pool_bd, row(pool_scale[l]), cc_dw_w[l], row(cc_dw_b[l]), row(cc_ln_g[l]), row(cc_ln_b[l]),
            cc_pw_w[l].astype(BF16), w_out[l].astype(BF16), row(ln_g[l]), row(ln_b[l]),
        )
        kbd, vbd = _kv_call(mem, wkt, wv)
        x = _layer_call(x, kbd, vbd, params, alpha)
    return x
```

```python
import functools

import jax
import jax.numpy as jnp
from jax import lax
from jax.experimental import pallas as pl
from jax.experimental.pallas import tpu as pltpu

D_MODEL = 1024
D_G = D_MODEL // 4
N_SUB = 4
HEAD_DIM = D_G // N_SUB
D_MIX = 5 * D_G
D_IN = 9 * D_G + D_MIX
CHUNK = 128
CONV_A = 3
CONV_D = 31
POOL_WINDOWS = (2, 4, 8, 16)
LN_EPS = 1e-5

OFF_A = 0
OFF_B = 3 * D_G
OFF_C = 5 * D_G
OFF_D = 6 * D_G
OFF_E = 8 * D_G
OFF_GATE = 9 * D_G

HALO_A = 8
HALO_C = 16
HALO_D = 32
CONV_ROWS = 32

PROJ_GROUPS = ((OFF_A, 3 * D_G), (OFF_B, 2 * D_G), (OFF_C, D_G), (OFF_D, 2 * D_G), (OFF_E, D_G)) + tuple(
    (OFF_GATE + g * D_G, D_G) for g in range(5))

SEQ_TILE = 256
VMEM_LIMIT_BYTES = 48 * 1024 * 1024

F32 = jnp.float32
BF16 = jnp.bfloat16


def _dot(a, b):
    return jnp.dot(a, b, preferred_element_type=F32)


def _layer_norm(x, g, b):
    mu = jnp.mean(x, axis=-1, keepdims=True)
    xc = x - mu
    var = jnp.mean(xc * xc, axis=-1, keepdims=True)
    return xc * lax.rsqrt(var + LN_EPS) * g + b


def _kv_kernel(mem_ref, wkt_ref, wv_ref, kbd_ref, vbd_ref):
    m = mem_ref[0].astype(BF16)
    mem_len = m.shape[0]
    kt = lax.dot_general(wkt_ref[...], m, (((1,), (1,)), ((), ())),
                         preferred_element_type=F32)
    kt = kt * (1.0 / (HEAD_DIM ** 0.5))
    v = _dot(m, wv_ref[...])
    row_head = lax.broadcasted_iota(jnp.int32, kt.shape, 0) // HEAD_DIM
    col_head = lax.broadcasted_iota(jnp.int32, v.shape, 1) // HEAD_DIM
    for h in range(N_SUB):
        kbd_ref[0, :, h * mem_len:(h + 1) * mem_len] = jnp.where(row_head == h, kt, 0.0).astype(BF16)
        vbd_ref[0, h * mem_len:(h + 1) * mem_len, :] = jnp.where(col_head == h, v, 0.0).astype(BF16)


def _layer_kernel(xp_ref, xn_ref, kbd_ref, vbd_ref, w_in_ref, conv_a_ref, sg_g_ref, sg_b_ref, sg_w_ref,
                  sg_bias_ref, pool_w_ref, pool_scale_ref, dw_w_ref, dw_b_ref, cc_g_ref, cc_b_ref,
                  pw_ref, w_out_ref, ln_g_ref, ln_b_ref, o_ref, xb_scr, hcat_scr, cx_scr, xc_scr, h_scr,
                  *proj_scrs, tile, n_seq, alpha):
    g = pl.program_id(0)
    s = g % n_seq
    proj_of = dict(zip(PROJ_GROUPS, proj_scrs))

    @pl.when(g == 0)
    def _():
        xb0 = xp_ref[0].astype(BF16)
        for (lo, width), scr in proj_of.items():
            scr[...] = _dot(xb0, w_in_ref[:, lo:lo + width])
        hcat_scr[...] = jnp.zeros(hcat_scr.shape, BF16)

    @pl.when(s == 0)
    def _():
        cx_scr[0:HALO_A, :] = jnp.zeros((HALO_A, D_G), F32)
        xc_scr[0:HALO_C, :] = jnp.zeros((HALO_C, D_G), F32)
        h_scr[0:HALO_D, :] = jnp.zeros((HALO_D, D_G), F32)

    def proj_tile(lo):
        group = [grp for grp in PROJ_GROUPS if grp[0] <= lo < grp[0] + grp[1]][0]
        return proj_of[group], lo - group[0]

    def take(lo, width):
        scr, off = proj_tile(lo)
        return scr[:, off:off + width]

    refillable = []

    def refill(*tiles):
        refillable.extend(tiles)
        while len(refillable) >= 2:
            pair = (refillable.pop(0), refillable.pop(0))
            w = jnp.concatenate([w_in_ref[:, t:t + D_G] for t in pair], axis=1)
            res = _dot(xb_scr[...], w)
            for i, t in enumerate(pair):
                dst, dst_off = proj_tile(t)
                dst[:, dst_off:dst_off + D_G] = res[:, i * D_G:(i + 1) * D_G]

    def gate(lo):
        return jax.nn.silu(take(OFF_GATE + lo, D_G))

    xb_scr[...] = xn_ref[0].astype(BF16)

    for lo in range(0, D_MODEL, 2 * D_G):
        cols = slice(lo, lo + 2 * D_G)
        o_ref[0, :, cols] = alpha * xp_ref[0, :, cols] + _dot(hcat_scr[...], w_out_ref[:, cols])

    pd = take(OFF_D, 2 * D_G)
    h_scr[HALO_D:HALO_D + tile, :] = pd[:, :D_G] * jax.nn.sigmoid(pd[:, D_G:])

    pa = take(OFF_A, 3 * D_G)
    xa, ba, ca = pa[:, :D_G], pa[:, D_G:2 * D_G], pa[:, 2 * D_G:]
    cx = ca * xa
    cx_scr[HALO_A:HALO_A + tile, :] = cx
    conv_a = conv_a_ref[2:3, :] * cx
    for k in range(CONV_A - 1):
        off = HALO_A - (CONV_A - 1) + k
        conv_a = conv_a + conv_a_ref[k:k + 1, :] * cx_scr[off:off + tile, :]
    h_a = (ba * conv_a * gate(0)).astype(BF16)
    cx_scr[0:HALO_A, :] = cx_scr[tile:tile + HALO_A, :]
    q = take(OFF_E, D_G).astype(BF16)

    lead = HALO_D - (CONV_D - 1)
    conv_d = []
    for c in range(tile // CONV_ROWS):
        base = c * CONV_ROWS
        acc = jnp.broadcast_to(dw_b_ref[...], (CONV_ROWS, D_G))
        for r in range(8):
            rows = CONV_ROWS + (8 if r else 0)
            part = None
            for j in range(r, lead + CONV_D, 8):
                if j < lead:
                    continue
                term = dw_w_ref[j - lead:j - lead + 1, :] * h_scr[base + j - r:base + j - r + rows, :]
                part = term if part is None else part + term
            acc = acc + part[r:r + CONV_ROWS, :]
        conv_d.append(acc)
    h_scr[0:HALO_D, :] = h_scr[tile:tile + HALO_D, :]
    refill(OFF_D, OFF_D + D_G, OFF_A, OFF_A + D_G)
    hn = jax.nn.silu(_layer_norm(jnp.concatenate(conv_d, axis=0), cc_g_ref[...], cc_b_ref[...]))
    y_d = _dot(hn.astype(BF16), pw_ref[...])
    scores = _dot(q, kbd_ref[0])

    o_ref[0] = _layer_norm(o_ref[0], ln_g_ref[...], ln_b_ref[...])
    refill(OFF_A + 2 * D_G, OFF_GATE)

    h_d = (y_d * gate(3 * D_G)).astype(BF16)

    mem_len = scores.shape[1] // N_SUB
    probs = []
    for h in range(N_SUB):
        sh = scores[:, h * mem_len:(h + 1) * mem_len]
        e = jnp.exp(sh - jnp.max(sh, axis=-1, keepdims=True))
        inv = 1.0 / jnp.sum(e, axis=-1, keepdims=True)
        probs.append((e * inv).astype(BF16))
    refill(OFF_GATE + 3 * D_G, OFF_E)
    y_e = _dot(jnp.concatenate(probs, axis=1), vbd_ref[0])

    pb = take(OFF_B, 2 * D_G)
    u = jax.nn.gelu(pb[:, :D_G])
    v = _layer_norm(jax.nn.gelu(pb[:, D_G:]), sg_g_ref[...], sg_b_ref[...])
    vb = v.astype(BF16)
    refill(OFF_B, OFF_B + D_G)
    wt = lax.broadcasted_iota(jnp.int32, (CHUNK, N_SUB * CHUNK), 0)
    ws = lax.broadcasted_iota(jnp.int32, (CHUNK, N_SUB * CHUNK), 1) % CHUNK
    w_mix = jnp.where(ws <= wt, sg_w_ref[...], 0.0).astype(BF16)
    lane_head = lax.broadcasted_iota(jnp.int32, (CHUNK, D_G), 1) // HEAD_DIM
    mixed = []
    for c in range(tile // CHUNK):
        vc = vb[c * CHUNK:(c + 1) * CHUNK, :]
        rhs = jnp.concatenate(
            [jnp.where(lane_head == h, vc, jnp.zeros_like(vc)) for h in range(N_SUB)], axis=0)
        mixed.append(_dot(w_mix, rhs) + sg_bias_ref[...])
    y_b = u * jnp.concatenate(mixed, axis=0)

    xc = take(OFF_C, D_G)
    xc_scr[HALO_C:HALO_C + tile, :] = xc

    xe = xc_scr[...]

    def doubled(val, steps):
        for shift in steps:
            val = val + pltpu.roll(val, shift, axis=0)
        return val

    s2 = doubled(xe[:, :128], (1,))
    s4 = doubled(s2, (2,))
    s8 = doubled(xe[:, 128:], (1, 2, 4))
    s16 = doubled(s8, (8,))
    lane = lax.broadcasted_iota(jnp.int32, s2.shape, 1)
    win_sum = jnp.concatenate([jnp.where(lane < HEAD_DIM, s2, s4),
                               jnp.where(lane < HEAD_DIM, s8, s16)], axis=1)[HALO_C:, :]
    t_abs = s * tile + lax.broadcasted_iota(jnp.int32, (tile, D_G), 0)
    lane_g = lax.broadcasted_iota(jnp.int32, (tile, D_G), 1) // HEAD_DIM
    window = jnp.left_shift(2, lane_g)
    count = jnp.minimum(t_abs + 1, window).astype(F32)
    pooled = (win_sum / count - xc).astype(BF16)
    y_c = _dot(pooled, pool_w_ref[...]) * pool_scale_ref[...]
    xc_scr[0:HALO_C, :] = xc_scr[tile:tile + HALO_C, :]

    h_e = (y_e * gate(4 * D_G)).astype(BF16)
    refill(OFF_C, OFF_GATE + 4 * D_G)
    h_b = (y_b * gate(D_G)).astype(BF16)
    h_c = (y_c * gate(2 * D_G)).astype(BF16)
    refill(OFF_GATE + D_G, OFF_GATE + 2 * D_G)
    assert not refillable
    hcat_scr[...] = jnp.concatenate([h_a, h_b, h_c, h_d, h_e], axis=1)


def _kv_call(mem, wkt, wv):
    batch, mem_len, _ = mem.shape
    return pl.pallas_call(
        _kv_kernel,
        grid=(batch,),
        in_specs=[
            pl.BlockSpec((1, mem_len, D_MODEL), lambda b: (b, 0, 0)),
            pl.BlockSpec((D_G, D_MODEL), lambda b: (0, 0)),
            pl.BlockSpec((D_MODEL, D_G), lambda b: (0, 0)),
        ],
        out_specs=[
            pl.BlockSpec((1, D_G, N_SUB * mem_len), lambda b: (b, 0, 0)),
            pl.BlockSpec((1, N_SUB * mem_len, D_G), lambda b: (b, 0, 0)),
        ],
        out_shape=[
            jax.ShapeDtypeStruct((batch, D_G, N_SUB * mem_len), BF16),
            jax.ShapeDtypeStruct((batch, N_SUB * mem_len, D_G), BF16),
        ],
        compiler_params=pltpu.CompilerParams(dimension_semantics=("arbitrary",)),
        name="mem_kv",
    )(mem, wkt, wv)


def _layer_call(x, kbd, vbd, params, alpha):
    batch, seq, _ = x.shape
    tile = SEQ_TILE
    assert seq % tile == 0 and tile % CHUNK == 0 and tile % CONV_ROWS == 0
    n_seq = seq // tile
    n_tiles = batch * n_seq
    kv_width = kbd.shape[2]

    def tile_at(g):
        g = jnp.clip(g, 0, n_tiles - 1)
        return g // n_seq, g % n_seq

    def const_spec(shape):
        return pl.BlockSpec(shape, lambda g: (0,) * len(shape))

    in_specs = [
        pl.BlockSpec((1, tile, D_MODEL), lambda g: (*tile_at(g - 1), 0)),
        pl.BlockSpec((1, tile, D_MODEL), lambda g: (*tile_at(g + 1), 0)),
        pl.BlockSpec((1, D_G, kv_width), lambda g: (tile_at(g)[0], 0, 0)),
        pl.BlockSpec((1, kv_width, D_G), lambda g: (tile_at(g)[0], 0, 0)),
    ] + [const_spec(p.shape) for p in params]
    return pl.pallas_call(
        functools.partial(_layer_kernel, tile=tile, n_seq=n_seq, alpha=alpha),
        grid=(n_tiles + 1,),
        in_specs=in_specs,
        out_specs=pl.BlockSpec((1, tile, D_MODEL), lambda g: (*tile_at(g - 1), 0)),
        out_shape=jax.ShapeDtypeStruct(x.shape, x.dtype),
        scratch_shapes=[
            pltpu.VMEM((tile, D_MODEL), BF16),
            pltpu.VMEM((tile, D_MIX), BF16),
            pltpu.VMEM((HALO_A + tile, D_G), F32),
            pltpu.VMEM((HALO_C + tile, D_G), F32),
            pltpu.VMEM((HALO_D + tile, D_G), F32),
        ] + [pltpu.VMEM((tile, width), F32) for _, width in PROJ_GROUPS],
        compiler_params=pltpu.CompilerParams(
            dimension_semantics=("arbitrary",),
            vmem_limit_bytes=VMEM_LIMIT_BYTES),
        name="mixer_layer",
    )(x, x, kbd, vbd, *params)


def kernel(x, mem, w_in, conv_a_w, sg_ln_g, sg_ln_b, sg_w, sg_b, pool_w, pool_scale, cc_dw_w, cc_dw_b, cc_ln_g, cc_ln_b, cc_pw_w, w_kv, w_out, ln_g, ln_b):
    depth = w_in.shape[0]
    alpha = (2.0 * depth) ** 0.25
    row = lambda a: a.reshape(1, -1)
    for l in range(depth):
        wkt = w_kv[l][:, :D_G].T.astype(BF16)
        wv = w_kv[l][:, D_G:].astype(BF16)
        sg_w_cat = sg_w[l].transpose(1, 0, 2).reshape(CHUNK, N_SUB * CHUNK)
        sg_bias = jnp.repeat(sg_b[l].T, HEAD_DIM, axis=1)
        pool_bd = jax.scipy.linalg.block_diag(*[pool_w[l, g] for g in range(N_SUB)]).astype(BF16)
        params = (
            w_in[l].astype(BF16), conv_a_w[l], row(sg_ln_g[l]), row(sg_ln_b[l]), sg_w_cat, sg_bias,
            pool_bd, row(pool_scale[l]), cc_dw_w[l], row(cc_dw_b[l]), row(cc_ln_g[l]), row(cc_ln_b[l]),
            cc_pw_w[l].astype(BF16), w_out[l].astype(BF16), row(ln_g[l]), row(ln_b[l]),
        )
        kbd, vbd = _kv_call(mem, wkt, wv)
        x = _layer_call(x, kbd, vbd, params, alpha)
    return x
```

```python
import functools

import jax
import jax.numpy as jnp
from jax import lax
from jax.experimental import pallas as pl
from jax.experimental.pallas import tpu as pltpu

D_MODEL = 1024
D_G = D_MODEL // 4
N_SUB = 4
HEAD_DIM = D_G // N_SUB
D_MIX = 5 * D_G
D_IN = 9 * D_G + D_MIX
CHUNK = 128
CONV_A = 3
CONV_D = 31
POOL_WINDOWS = (2, 4, 8, 16)
LN_EPS = 1e-5

OFF_A = 0
OFF_B = 3 * D_G
OFF_C = 5 * D_G
OFF_D = 6 * D_G
OFF_E = 8 * D_G
OFF_GATE = 9 * D_G

HALO_A = 8
HALO_C = 16
HALO_D = 32
CONV_ROWS = 32

PROJ_GROUPS = ((OFF_A, 3 * D_G), (OFF_B, 2 * D_G), (OFF_C, D_G), (OFF_D, 2 * D_G), (OFF_E, D_G)) + tuple(
    (OFF_GATE + g * D_G, D_G) for g in range(5))

SEQ_TILE = 256
VMEM_LIMIT_BYTES = 48 * 1024 * 1024

F32 = jnp.float32
BF16 = jnp.bfloat16


def _dot(a, b):
    return jnp.dot(a, b, preferred_element_type=F32)


def _layer_norm(x, g, b):
    mu = jnp.mean(x, axis=-1, keepdims=True)
    xc = x - mu
    var = jnp.mean(xc * xc, axis=-1, keepdims=True)
    return xc * lax.rsqrt(var + LN_EPS) * g + b


def _sequenced_after(x, done):
    words = pltpu.bitcast(done, jnp.uint32)
    word = words[0:8, 0:128]
    for i in range(0, words.shape[0], 8):
        for j in range(0, words.shape[1], 128):
            if i or j:
                word = word + words[i:i + 8, j:j + 128]
    word = jnp.max(word.astype(jnp.int32), axis=-1, keepdims=True).astype(jnp.uint32)
    zero = lax.shift_right_logical(lax.shift_right_logical(word, jnp.uint32(16)), jnp.uint32(16))
    return pltpu.bitcast(pltpu.bitcast(x, jnp.uint32) | zero, F32)


def _kv_kernel(mem_ref, wkt_ref, wv_ref, kbd_ref, vbd_ref):
    m = mem_ref[0].astype(BF16)
    mem_len = m.shape[0]
    kt = lax.dot_general(wkt_ref[...], m, (((1,), (1,)), ((), ())),
                         preferred_element_type=F32)
    kt = kt * (1.0 / (HEAD_DIM ** 0.5))
    v = _dot(m, wv_ref[...])
    row_head = lax.broadcasted_iota(jnp.int32, kt.shape, 0) // HEAD_DIM
    col_head = lax.broadcasted_iota(jnp.int32, v.shape, 1) // HEAD_DIM
    for h in range(N_SUB):
        kbd_ref[0, :, h * mem_len:(h + 1) * mem_len] = jnp.where(row_head == h, kt, 0.0).astype(BF16)
        vbd_ref[0, h * mem_len:(h + 1) * mem_len, :] = jnp.where(col_head == h, v, 0.0).astype(BF16)


def _layer_kernel(xp_ref, xn_ref, kbd_ref, vbd_ref, w_in_ref, conv_a_ref, sg_g_ref, sg_b_ref, sg_w_ref,
                  sg_bias_ref, pool_w_ref, pool_scale_ref, dw_w_ref, dw_b_ref, cc_g_ref, cc_b_ref,
                  pw_ref, w_out_ref, ln_g_ref, ln_b_ref, o_ref, xb_scr, hcat_scr, cx_scr, xc_scr, h_scr,
                  *proj_scrs, tile, n_seq, alpha):
    g = pl.program_id(0)
    s = g % n_seq
    proj_of = dict(zip(PROJ_GROUPS, proj_scrs))

    @pl.when(g == 0)
    def _():
        xb0 = xp_ref[0].astype(BF16)
        for (lo, width), scr in proj_of.items():
            scr[...] = _dot(xb0, w_in_ref[:, lo:lo + width])
        hcat_scr[...] = jnp.zeros(hcat_scr.shape, BF16)

    @pl.when(s == 0)
    def _():
        cx_scr[0:HALO_A, :] = jnp.zeros((HALO_A, D_G), F32)
        xc_scr[0:HALO_C, :] = jnp.zeros((HALO_C, D_G), F32)
        h_scr[0:HALO_D, :] = jnp.zeros((HALO_D, D_G), F32)

    def proj_tile(lo):
        group = [grp for grp in PROJ_GROUPS if grp[0] <= lo < grp[0] + grp[1]][0]
        return proj_of[group], lo - group[0]

    def take(lo, width):
        scr, off = proj_tile(lo)
        return scr[:, off:off + width]

    refillable = []

    def refill(*tiles):
        refillable.extend(tiles)
        while len(refillable) >= 2:
            pair = (refillable.pop(0), refillable.pop(0))
            w = jnp.concatenate([w_in_ref[:, t:t + D_G] for t in pair], axis=1)
            res = _dot(xb_scr[...], w)
            for i, t in enumerate(pair):
                dst, dst_off = proj_tile(t)
                dst[:, dst_off:dst_off + D_G] = res[:, i * D_G:(i + 1) * D_G]

    def gate(lo):
        return jax.nn.silu(take(OFF_GATE + lo, D_G))

    xb_scr[...] = xn_ref[0].astype(BF16)

    for lo in range(0, D_MODEL, 2 * D_G):
        cols = slice(lo, lo + 2 * D_G)
        o_ref[0, :, cols] = _dot(hcat_scr[...], w_out_ref[:, cols])

    pd = take(OFF_D, 2 * D_G)
    h_scr[HALO_D:HALO_D + tile, :] = pd[:, :D_G] * jax.nn.sigmoid(pd[:, D_G:])

    pa = take(OFF_A, 3 * D_G)
    xa, ba, ca = pa[:, :D_G], pa[:, D_G:2 * D_G], pa[:, 2 * D_G:]
    cx = ca * xa
    cx_scr[HALO_A:HALO_A + tile, :] = cx
    conv_a = conv_a_ref[2:3, :] * cx
    for k in range(CONV_A - 1):
        off = HALO_A - (CONV_A - 1) + k
        conv_a = conv_a + conv_a_ref[k:k + 1, :] * cx_scr[off:off + tile, :]
    h_a = (ba * conv_a * gate(0)).astype(BF16)
    cx_scr[0:HALO_A, :] = cx_scr[tile:tile + HALO_A, :]
    q = take(OFF_E, D_G).astype(BF16)

    lead = HALO_D - (CONV_D - 1)
    conv_d = []
    for c in range(tile // CONV_ROWS):
        base = c * CONV_ROWS
        bias = jnp.broadcast_to(dw_b_ref[...], (8, D_G))
        if conv_d:
            bias = _sequenced_after(bias, conv_d[-1])
        acc = jnp.concatenate([bias] * (CONV_ROWS // 8), axis=0)
        for r in range(8):
            rows = CONV_ROWS + (8 if r else 0)
            part = None
            for j in range(r, lead + CONV_D, 8):
                if j < lead:
                    continue
                term = dw_w_ref[j - lead:j - lead + 1, :] * h_scr[base + j - r:base + j - r + rows, :]
                part = term if part is None else part + term
            acc = acc + part[r:r + CONV_ROWS, :]
        conv_d.append(acc)
    h_scr[0:HALO_D, :] = h_scr[tile:tile + HALO_D, :]
    refill(OFF_D, OFF_D + D_G, OFF_A, OFF_A + D_G)
    hn = jax.nn.silu(_layer_norm(jnp.concatenate(conv_d, axis=0), cc_g_ref[...], cc_b_ref[...]))
    y_d = _dot(hn.astype(BF16), pw_ref[...])
    scores = _dot(q, kbd_ref[0])

    o_ref[0] = _layer_norm(alpha * xp_ref[0] + o_ref[0], ln_g_ref[...], ln_b_ref[...])
    refill(OFF_A + 2 * D_G, OFF_GATE)

    h_d = (y_d * gate(3 * D_G)).astype(BF16)

    mem_len = scores.shape[1] // N_SUB
    probs = []
    for h in range(N_SUB):
        sh = scores[:, h * mem_len:(h + 1) * mem_len]
        e = jnp.exp(sh - jnp.max(sh, axis=-1, keepdims=True))
        inv = 1.0 / jnp.sum(e, axis=-1, keepdims=True)
        probs.append((e * inv).astype(BF16))
    refill(OFF_GATE + 3 * D_G, OFF_E)
    y_e = _dot(jnp.concatenate(probs, axis=1), vbd_ref[0])

    pb = take(OFF_B, 2 * D_G)
    u = jax.nn.gelu(pb[:, :D_G])
    v = _layer_norm(jax.nn.gelu(pb[:, D_G:]), sg_g_ref[...], sg_b_ref[...])
    vb = v.astype(BF16)
    refill(OFF_B, OFF_B + D_G)
    wt = lax.broadcasted_iota(jnp.int32, (CHUNK, N_SUB * CHUNK), 0)
    ws = lax.broadcasted_iota(jnp.int32, (CHUNK, N_SUB * CHUNK), 1) % CHUNK
    w_mix = jnp.where(ws <= wt, sg_w_ref[...], 0.0).astype(BF16)
    lane_head = lax.broadcasted_iota(jnp.int32, (CHUNK, D_G), 1) // HEAD_DIM
    mixed = []
    for c in range(tile // CHUNK):
        vc = vb[c * CHUNK:(c + 1) * CHUNK, :]
        rhs = jnp.concatenate(
            [jnp.where(lane_head == h, vc, jnp.zeros_like(vc)) for h in range(N_SUB)], axis=0)
        mixed.append(_dot(w_mix, rhs) + sg_bias_ref[...])
    y_b = u * jnp.concatenate(mixed, axis=0)

    xc = take(OFF_C, D_G)
    xc_scr[HALO_C:HALO_C + tile, :] = xc

    xe = xc_scr[...]

    def doubled(val, steps):
        for shift in steps:
            val = val + pltpu.roll(val, shift, axis=0)
        return val

    s2 = doubled(xe[:, :128], (1,))
    s4 = doubled(s2, (2,))
    s8 = doubled(xe[:, 128:], (1, 2, 4))
    s16 = doubled(s8, (8,))
    lane = lax.broadcasted_iota(jnp.int32, s2.shape, 1)
    win_sum = jnp.concatenate([jnp.where(lane < HEAD_DIM, s2, s4),
                               jnp.where(lane < HEAD_DIM, s8, s16)], axis=1)[HALO_C:, :]
    t_abs = s * tile + lax.broadcasted_iota(jnp.int32, (tile, D_G), 0)
    lane_g = lax.broadcasted_iota(jnp.int32, (tile, D_G), 1) // HEAD_DIM
    window = jnp.left_shift(2, lane_g)
    count = jnp.minimum(t_abs + 1, window).astype(F32)
    pooled = (win_sum / count - xc).astype(BF16)
    y_c = _dot(pooled, pool_w_ref[...]) * pool_scale_ref[...]
    xc_scr[0:HALO_C, :] = xc_scr[tile:tile + HALO_C, :]

    h_e = (y_e * gate(4 * D_G)).astype(BF16)
    refill(OFF_C, OFF_GATE + 4 * D_G)
    h_b = (y_b * gate(D_G)).astype(BF16)
    h_c = (y_c * gate(2 * D_G)).astype(BF16)
    refill(OFF_GATE + D_G, OFF_GATE + 2 * D_G)
    assert not refillable
    hcat_scr[...] = jnp.concatenate([h_a, h_b, h_c, h_d, h_e], axis=1)


def _kv_call(mem, wkt, wv):
    batch, mem_len, _ = mem.shape
    return pl.pallas_call(
        _kv_kernel,
        grid=(batch,),
        in_specs=[
            pl.BlockSpec((1, mem_len, D_MODEL), lambda b: (b, 0, 0)),
            pl.BlockSpec((D_G, D_MODEL), lambda b: (0, 0)),
            pl.BlockSpec((D_MODEL, D_G), lambda b: (0, 0)),
        ],
        out_specs=[
            pl.BlockSpec((1, D_G, N_SUB * mem_len), lambda b: (b, 0, 0)),
            pl.BlockSpec((1, N_SUB * mem_len, D_G), lambda b: (b, 0, 0)),
        ],
        out_shape=[
            jax.ShapeDtypeStruct((batch, D_G, N_SUB * mem_len), BF16),
            jax.ShapeDtypeStruct((batch, N_SUB * mem_len, D_G), BF16),
        ],
        compiler_params=pltpu.CompilerParams(dimension_semantics=("arbitrary",)),
        name="mem_kv",
    )(mem, wkt, wv)


def _layer_call(x, kbd, vbd, params, alpha):
    batch, seq, _ = x.shape
    tile = SEQ_TILE
    assert seq % tile == 0 and tile % CHUNK == 0 and tile % CONV_ROWS == 0
    n_seq = seq // tile
    n_tiles = batch * n_seq
    kv_width = kbd.shape[2]

    def tile_at(g):
        g = jnp.clip(g, 0, n_tiles - 1)
        return g // n_seq, g % n_seq

    def const_spec(shape):
        return pl.BlockSpec(shape, lambda g: (0,) * len(shape))

    in_specs = [
        pl.BlockSpec((1, tile, D_MODEL), lambda g: (*tile_at(g - 1), 0)),
        pl.BlockSpec((1, tile, D_MODEL), lambda g: (*tile_at(g + 1), 0)),
        pl.BlockSpec((1, D_G, kv_width), lambda g: (tile_at(g)[0], 0, 0)),
        pl.BlockSpec((1, kv_width, D_G), lambda g: (tile_at(g)[0], 0, 0)),
    ] + [const_spec(p.shape) for p in params]
    return pl.pallas_call(
        functools.partial(_layer_kernel, tile=tile, n_seq=n_seq, alpha=alpha),
        grid=(n_tiles + 1,),
        in_specs=in_specs,
        out_specs=pl.BlockSpec((1, tile, D_MODEL), lambda g: (*tile_at(g - 1), 0)),
        out_shape=jax.ShapeDtypeStruct(x.shape, x.dtype),
        scratch_shapes=[
            pltpu.VMEM((tile, D_MODEL), BF16),
            pltpu.VMEM((tile, D_MIX), BF16),
            pltpu.VMEM((HALO_A + tile, D_G), F32),
            pltpu.VMEM((HALO_C + tile, D_G), F32),
            pltpu.VMEM((HALO_D + tile, D_G), F32),
        ] + [pltpu.VMEM((tile, width), F32) for _, width in PROJ_GROUPS],
        compiler_params=pltpu.CompilerParams(
            dimension_semantics=("arbitrary",),
            vmem_limit_bytes=VMEM_LIMIT_BYTES),
        name="mixer_layer",
    )(x, x, kbd, vbd, *params)


def kernel(x, mem, w_in, conv_a_w, sg_ln_g, sg_ln_b, sg_w, sg_b, pool_w, pool_scale, cc_dw_w, cc_dw_b, cc_ln_g, cc_ln_b, cc_pw_w, w_kv, w_out, ln_g, ln_b):
    depth = w_in.shape[0]
    alpha = (2.0 * depth) ** 0.25
    row = lambda a: a.reshape(1, -1)
    for l in range(depth):
        wkt = w_kv[l][:, :D_G].T.astype(BF16)
        wv = w_kv[l][:, D_G:].astype(BF16)
        sg_w_cat = sg_w[l].transpose(1, 0, 2).reshape(CHUNK, N_SUB * CHUNK)
        sg_bias = jnp.repeat(sg_b[l].T, HEAD_DIM, axis=1)
        pool_bd = jax.scipy.linalg.block_diag(*[pool_w[l, g] for g in range(N_SUB)]).astype(BF16)
        params = (
            w_in[l].astype(BF16), conv_a_w[l], row(sg_ln_g[l]), row(sg_ln_b[l]), sg_w_cat, sg_bias,
            pool_bd, row(pool_scale[l]), cc_dw_w[l], row(cc_dw_b[l]), row(cc_ln_g[l]), row(cc_ln_b[l]),
            cc_pw_w[l].astype(BF16), w_out[l].astype(BF16), row(ln_g[l]), row(ln_b[l]),
        )
        kbd, vbd = _kv_call(mem, wkt, wv)
        x = _layer_call(x, kbd, vbd, params, alpha)
    return x
```

```python
import functools

import jax
import jax.numpy as jnp
from jax import lax
from jax.experimental import pallas as pl
from jax.experimental.pallas import tpu as pltpu

D_MODEL = 1024
D_G = D_MODEL // 4
N_SUB = 4
HEAD_DIM = D_G // N_SUB
D_MIX = 5 * D_G
D_IN = 9 * D_G + D_MIX
CHUNK = 128
CONV_A = 3
CONV_D = 31
POOL_WINDOWS = (2, 4, 8, 16)
LN_EPS = 1e-5

OFF_A = 0
OFF_B = 3 * D_G
OFF_C = 5 * D_G
OFF_D = 6 * D_G
OFF_E = 8 * D_G
OFF_GATE = 9 * D_G

HALO_A = 8
HALO_C = 16
HALO_D = 32
CONV_ROWS = 32

PROJ_GROUPS = ((OFF_A, 3 * D_G), (OFF_B, 2 * D_G), (OFF_C, D_G), (OFF_D, 2 * D_G), (OFF_E, D_G)) + tuple(
    (OFF_GATE + g * D_G, D_G) for g in range(5))

SEQ_TILE = 512
VMEM_LIMIT_BYTES = 56 * 1024 * 1024

F32 = jnp.float32
BF16 = jnp.bfloat16


def _dot(a, b):
    return jnp.dot(a, b, preferred_element_type=F32)


def _layer_norm(x, g, b):
    mu = jnp.mean(x, axis=-1, keepdims=True)
    xc = x - mu
    var = jnp.mean(xc * xc, axis=-1, keepdims=True)
    return xc * lax.rsqrt(var + LN_EPS) * g + b


def _sequenced_after(x, done):
    words = pltpu.bitcast(done, jnp.uint32)
    word = words[0:8, 0:128]
    for i in range(0, words.shape[0], 8):
        for j in range(0, words.shape[1], 128):
            if i or j:
                word = word + words[i:i + 8, j:j + 128]
    word = jnp.max(word.astype(jnp.int32), axis=-1, keepdims=True).astype(jnp.uint32)
    zero = lax.shift_right_logical(lax.shift_right_logical(word, jnp.uint32(16)), jnp.uint32(16))
    return pltpu.bitcast(pltpu.bitcast(x, jnp.uint32) | zero, F32)


def _kv_kernel(mem_ref, wkt_ref, wv_ref, kbd_ref, vbd_ref):
    m = mem_ref[0].astype(BF16)
    mem_len = m.shape[0]
    kt = lax.dot_general(wkt_ref[...], m, (((1,), (1,)), ((), ())),
                         preferred_element_type=F32)
    kt = kt * (1.0 / (HEAD_DIM ** 0.5))
    v = _dot(m, wv_ref[...])
    row_head = lax.broadcasted_iota(jnp.int32, kt.shape, 0) // HEAD_DIM
    col_head = lax.broadcasted_iota(jnp.int32, v.shape, 1) // HEAD_DIM
    for h in range(N_SUB):
        kbd_ref[0, :, h * mem_len:(h + 1) * mem_len] = jnp.where(row_head == h, kt, 0.0).astype(BF16)
        vbd_ref[0, h * mem_len:(h + 1) * mem_len, :] = jnp.where(col_head == h, v, 0.0).astype(BF16)


def _layer_kernel(xp_ref, xn_ref, kbd_ref, vbd_ref, w_in_ref, conv_a_ref, sg_g_ref, sg_b_ref, sg_w_ref,
                  sg_bias_ref, pool_w_ref, pool_scale_ref, dw_w_ref, dw_b_ref, cc_g_ref, cc_b_ref,
                  pw_ref, w_out_ref, ln_g_ref, ln_b_ref, o_ref, xb_scr, hcat_scr, cx_scr, xc_scr, h_scr,
                  *proj_scrs, tile, n_seq, alpha):
    g = pl.program_id(0)
    s = g % n_seq
    proj_of = dict(zip(PROJ_GROUPS, proj_scrs))

    @pl.when(g == 0)
    def _():
        xb0 = xp_ref[0].astype(BF16)
        for (lo, width), scr in proj_of.items():
            scr[...] = _dot(xb0, w_in_ref[:, lo:lo + width])
        hcat_scr[...] = jnp.zeros(hcat_scr.shape, BF16)

    @pl.when(s == 0)
    def _():
        cx_scr[0:HALO_A, :] = jnp.zeros((HALO_A, D_G), F32)
        xc_scr[0:HALO_C, :] = jnp.zeros((HALO_C, D_G), F32)
        h_scr[0:HALO_D, :] = jnp.zeros((HALO_D, D_G), F32)

    def proj_tile(lo):
        group = [grp for grp in PROJ_GROUPS if grp[0] <= lo < grp[0] + grp[1]][0]
        return proj_of[group], lo - group[0]

    def take(lo, width):
        scr, off = proj_tile(lo)
        return scr[:, off:off + width]

    refillable = []

    def refill(*tiles):
        refillable.extend(tiles)
        while len(refillable) >= 2:
            pair = (refillable.pop(0), refillable.pop(0))
            w = jnp.concatenate([w_in_ref[:, t:t + D_G] for t in pair], axis=1)
            res = _dot(xb_scr[...], w)
            for i, t in enumerate(pair):
                dst, dst_off = proj_tile(t)
                dst[:, dst_off:dst_off + D_G] = res[:, i * D_G:(i + 1) * D_G]

    def gate(lo):
        return jax.nn.silu(take(OFF_GATE + lo, D_G))

    xb_scr[...] = xn_ref[0].astype(BF16)

    for lo in range(0, D_MODEL, 2 * D_G):
        cols = slice(lo, lo + 2 * D_G)
        o_ref[0, :, cols] = _dot(hcat_scr[...], w_out_ref[:, cols])

    pd = take(OFF_D, 2 * D_G)
    h_scr[HALO_D:HALO_D + tile, :] = pd[:, :D_G] * jax.nn.sigmoid(pd[:, D_G:])

    pa = take(OFF_A, 3 * D_G)
    xa, ba, ca = pa[:, :D_G], pa[:, D_G:2 * D_G], pa[:, 2 * D_G:]
    cx = ca * xa
    cx_scr[HALO_A:HALO_A + tile, :] = cx
    conv_a = conv_a_ref[2:3, :] * cx
    for k in range(CONV_A - 1):
        off = HALO_A - (CONV_A - 1) + k
        conv_a = conv_a + conv_a_ref[k:k + 1, :] * cx_scr[off:off + tile, :]
    h_a = (ba * conv_a * gate(0)).astype(BF16)
    cx_scr[0:HALO_A, :] = cx_scr[tile:tile + HALO_A, :]
    q = take(OFF_E, D_G).astype(BF16)

    lead = HALO_D - (CONV_D - 1)
    conv_d = []
    for c in range(tile // CONV_ROWS):
        base = c * CONV_ROWS
        bias = jnp.broadcast_to(dw_b_ref[...], (8, D_G))
        if conv_d:
            bias = _sequenced_after(bias, conv_d[-1])
        acc = jnp.concatenate([bias] * (CONV_ROWS // 8), axis=0)
        for r in range(8):
            rows = CONV_ROWS + (8 if r else 0)
            part = None
            for j in range(r, lead + CONV_D, 8):
                if j < lead:
                    continue
                term = dw_w_ref[j - lead:j - lead + 1, :] * h_scr[base + j - r:base + j - r + rows, :]
                part = term if part is None else part + term
            acc = acc + part[r:r + CONV_ROWS, :]
        conv_d.append(acc)
    h_scr[0:HALO_D, :] = h_scr[tile:tile + HALO_D, :]
    refill(OFF_D, OFF_D + D_G, OFF_A, OFF_A + D_G)
    hn = jax.nn.silu(_layer_norm(jnp.concatenate(conv_d, axis=0), cc_g_ref[...], cc_b_ref[...]))
    y_d = _dot(hn.astype(BF16), pw_ref[...])
    scores = _dot(q, kbd_ref[0])
    refill(OFF_A + 2 * D_G, OFF_GATE)
    h_d =(y_d * gate(3 * D_G)).astype(BF16)

    mem_len = scores.shape[1] // N_SUB
    probs = []
    for h in range(N_SUB):
        sh = scores[:, h * mem_len:(h + 1) * mem_len]
        e = jnp.exp(sh - jnp.max(sh, axis=-1, keepdims=True))
        inv = 1.0 / jnp.sum(e, axis=-1, keepdims=True)
        probs.append((e * inv).astype(BF16))
    refill(OFF_GATE + 3 * D_G, OFF_E)
    y_e = _dot(jnp.concatenate(probs, axis=1), vbd_ref[0])

    pb = take(OFF_B, 2 * D_G)
    u = jax.nn.gelu(pb[:, :D_G])
    v = _layer_norm(jax.nn.gelu(pb[:, D_G:]), sg_g_ref[...], sg_b_ref[...])
    vb = v.astype(BF16)
    refill(OFF_B, OFF_B + D_G)
    wt = lax.broadcasted_iota(jnp.int32, (CHUNK, N_SUB * CHUNK), 0)
    ws = lax.broadcasted_iota(jnp.int32, (CHUNK, N_SUB * CHUNK), 1) % CHUNK
    w_mix = jnp.where(ws <= wt, sg_w_ref[...], 0.0).astype(BF16)
    lane_head = lax.broadcasted_iota(jnp.int32, (CHUNK, D_G), 1) // HEAD_DIM
    mixed = []
    for c in range(tile // CHUNK):
        vc = vb[c * CHUNK:(c + 1) * CHUNK, :]
        rhs = jnp.concatenate(
            [jnp.where(lane_head == h, vc, jnp.zeros_like(vc)) for h in range(N_SUB)], axis=0)
        mixed.append(_dot(w_mix, rhs) + sg_bias_ref[...])
    y_b = u * jnp.concatenate(mixed, axis=0)

    xc = take(OFF_C, D_G)
    xc_scr[HALO_C:HALO_C + tile, :] = xc

    xe = xc_scr[...]

    def doubled(val, steps):
        for shift in steps:
            val = val + pltpu.roll(val, shift, axis=0)
        return val

    s2 = doubled(xe[:, :128], (1,))
    s4 = doubled(s2, (2,))
    s8 = doubled(xe[:, 128:], (1, 2, 4))
    s16 = doubled(s8, (8,))
    lane = lax.broadcasted_iota(jnp.int32, s2.shape, 1)
    win_sum = jnp.concatenate([jnp.where(lane < HEAD_DIM, s2, s4),
                               jnp.where(lane < HEAD_DIM, s8, s16)], axis=1)[HALO_C:, :]
    t_abs = s * tile + lax.broadcasted_iota(jnp.int32, (tile, D_G), 0)
    lane_g = lax.broadcasted_iota(jnp.int32, (tile, D_G), 1) // HEAD_DIM
    window = jnp.left_shift(2, lane_g)
    count = jnp.minimum(t_abs + 1, window).astype(F32)
    pooled = (win_sum / count - xc).astype(BF16)
    y_c = _dot(pooled, pool_w_ref[...]) * pool_scale_ref[...]
    xc_scr[0:HALO_C, :] = xc_scr[tile:tile + HALO_C, :]

    h_e = (y_e * gate(4 * D_G)).astype(BF16)
    refill(OFF_C, OFF_GATE + 4 * D_G)
    h_b = (y_b * gate(D_G)).astype(BF16)
    h_c = (y_c * gate(2 * D_G)).astype(BF16)
    refill(OFF_GATE + D_G, OFF_GATE + 2 * D_G)
    assert not refillable
    hcat_scr[...] = jnp.concatenate([h_a, h_b, h_c, h_d, h_e], axis=1)

    alpha8 = _sequenced_after(jnp.full((8, 128), alpha, F32), y_c[tile - CONV_ROWS:, :])
    alpha_t = jnp.concatenate([jnp.concatenate([alpha8] * (D_MODEL // 128), axis=1)] * (tile // 8), axis=0)
    o_ref[0] = _layer_norm(alpha_t * xp_ref[0] + o_ref[0], ln_g_ref[...], ln_b_ref[...])


def _kv_call(mem, wkt, wv):
    batch, mem_len, _ = mem.shape
    return pl.pallas_call(
        _kv_kernel,
        grid=(batch,),
        in_specs=[
            pl.BlockSpec((1, mem_len, D_MODEL), lambda b: (b, 0, 0)),
            pl.BlockSpec((D_G, D_MODEL), lambda b: (0, 0)),
            pl.BlockSpec((D_MODEL, D_G), lambda b: (0, 0)),
        ],
        out_specs=[
            pl.BlockSpec((1, D_G, N_SUB * mem_len), lambda b: (b, 0, 0)),
            pl.BlockSpec((1, N_SUB * mem_len, D_G), lambda b: (b, 0, 0)),
        ],
        out_shape=[
            jax.ShapeDtypeStruct((batch, D_G, N_SUB * mem_len), BF16),
            jax.ShapeDtypeStruct((batch, N_SUB * mem_len, D_G), BF16),
        ],
        compiler_params=pltpu.CompilerParams(dimension_semantics=("arbitrary",)),
        name="mem_kv",
    )(mem, wkt, wv)


def _layer_call(x, kbd, vbd, params, alpha):
    batch, seq, _ = x.shape
    tile = SEQ_TILE
    assert seq % tile == 0 and tile % CHUNK == 0 and tile % CONV_ROWS == 0
    n_seq = seq // tile
    n_tiles = batch * n_seq
    kv_width = kbd.shape[2]

    def tile_at(g):
        g = jnp.clip(g, 0, n_tiles - 1)
        return g // n_seq, g % n_seq

    def const_spec(shape):
        return pl.BlockSpec(shape, lambda g: (0,) * len(shape), pipeline_mode=pl.Buffered(1))

    in_specs = [
        pl.BlockSpec((1, tile, D_MODEL), lambda g: (*tile_at(g - 1), 0)),
        pl.BlockSpec((1, tile, D_MODEL), lambda g: (*tile_at(g + 1), 0)),
        pl.BlockSpec((1, D_G, kv_width), lambda g: (tile_at(g)[0], 0, 0)),
        pl.BlockSpec((1, kv_width, D_G), lambda g: (tile_at(g)[0], 0, 0)),
    ] + [const_spec(p.shape) for p in params]
    return pl.pallas_call(
        functools.partial(_layer_kernel, tile=tile, n_seq=n_seq, alpha=alpha),
        grid=(n_tiles + 1,),
        in_specs=in_specs,
        out_specs=pl.BlockSpec((1, tile, D_MODEL), lambda g: (*tile_at(g - 1), 0)),
        out_shape=jax.ShapeDtypeStruct(x.shape, x.dtype),
        scratch_shapes=[
            pltpu.VMEM((tile, D_MODEL), BF16),
            pltpu.VMEM((tile, D_MIX), BF16),
            pltpu.VMEM((HALO_A + tile, D_G), F32),
            pltpu.VMEM((HALO_C + tile, D_G), F32),
            pltpu.VMEM((HALO_D + tile, D_G), F32),
        ] + [pltpu.VMEM((tile, width), F32) for _, width in PROJ_GROUPS],
        compiler_params=pltpu.CompilerParams(
            dimension_semantics=("arbitrary",),
            vmem_limit_bytes=VMEM_LIMIT_BYTES),
        name="mixer_layer",
    )(x, x, kbd, vbd, *params)


def kernel(x, mem, w_in, conv_a_w, sg_ln_g, sg_ln_b, sg_w, sg_b, pool_w, pool_scale, cc_dw_w, cc_dw_b, cc_ln_g, cc_ln_b, cc_pw_w, w_kv, w_out, ln_g, ln_b):
    depth = w_in.shape[0]
    alpha = (2.0 * depth) ** 0.25
    row = lambda a: a.reshape(1, -1)
    for l in range(depth):
        wkt = w_kv[l][:, :D_G].T.astype(BF16)
        wv = w_kv[l][:, D_G:].astype(BF16)
        sg_w_cat = sg_w[l].transpose(1, 0, 2).reshape(CHUNK, N_SUB * CHUNK)
        sg_bias = jnp.repeat(sg_b[l].T, HEAD_DIM, axis=1)
        pool_bd = jax.scipy.linalg.block_diag(*[pool_w[l, g] for g in range(N_SUB)]).astype(BF16)
        params = (
            w_in[l].astype(BF16), conv_a_w[l], row(sg_ln_g[l]), row(sg_ln_b[l]), sg_w_cat, sg_bias,
            pool_bd, row(pool_scale[l]), cc_dw_w[l], row(cc_dw_b[l]), row(cc_ln_g[l]), row(cc_ln_b[l]),
            cc_pw_w[l].astype(BF16), w_out[l].astype(BF16), row(ln_g[l]), row(ln_b[l]),
        )
        kbd, vbd = _kv_call(mem, wkt, wv)
        x = _layer_call(x, kbd, vbd, params, alpha)
    return x
```

```python
import functools

import jax
import jax.numpy as jnp
from jax import lax
from jax.experimental import pallas as pl
from jax.experimental.pallas import tpu as pltpu

D_MODEL = 1024
D_G = D_MODEL // 4
N_SUB = 4
HEAD_DIM = D_G // N_SUB
D_MIX = 5 * D_G
D_IN = 9 * D_G + D_MIX
CHUNK = 128
CONV_A = 3
CONV_D = 31
POOL_WINDOWS = (2, 4, 8, 16)
LN_EPS = 1e-5

OFF_A = 0
OFF_B = 3 * D_G
OFF_C = 5 * D_G
OFF_D = 6 * D_G
OFF_E = 8 * D_G
OFF_GATE = 9 * D_G

HALO_A = 8
HALO_C = 16
HALO_D = 32
CONV_ROWS = 32

PROJ_GROUPS = ((OFF_A, 3 * D_G), (OFF_B, 2 * D_G), (OFF_C, D_G), (OFF_D, 2 * D_G), (OFF_E, D_G)) + tuple(
    (OFF_GATE + g * D_G, D_G) for g in range(5))

SEQ_TILE = 512
VMEM_LIMIT_BYTES = 56 * 1024 * 1024

F32 = jnp.float32
BF16 = jnp.bfloat16


def _dot(a, b):
    return jnp.dot(a, b, preferred_element_type=F32)


def _layer_norm(x, g, b):
    mu = jnp.mean(x, axis=-1, keepdims=True)
    xc = x - mu
    var = jnp.mean(xc * xc, axis=-1, keepdims=True)
    return xc * lax.rsqrt(var + LN_EPS) * g + b


def _sequenced_after(x, done):
    words = pltpu.bitcast(done, jnp.uint32)
    word = words[0:8, 0:128]
    for i in range(0, words.shape[0], 8):
        for j in range(0, words.shape[1], 128):
            if i or j:
                word = word + words[i:i + 8, j:j + 128]
    word = jnp.max(word.astype(jnp.int32), axis=-1, keepdims=True).astype(jnp.uint32)
    zero = lax.shift_right_logical(lax.shift_right_logical(word, jnp.uint32(16)), jnp.uint32(16))
    return pltpu.bitcast(pltpu.bitcast(x, jnp.uint32) | zero, F32)


def _kv_kernel(mem_ref, wkt_ref, wv_ref, kbd_ref, vbd_ref):
    m = mem_ref[0].astype(BF16)
    mem_len = m.shape[0]
    kt = lax.dot_general(wkt_ref[...], m, (((1,), (1,)), ((), ())),
                         preferred_element_type=F32)
    kt = kt * (1.0 / (HEAD_DIM ** 0.5))
    v = _dot(m, wv_ref[...])
    row_head = lax.broadcasted_iota(jnp.int32, kt.shape, 0) // HEAD_DIM
    col_head = lax.broadcasted_iota(jnp.int32, v.shape, 1) // HEAD_DIM
    for h in range(N_SUB):
        kbd_ref[0, :, h * mem_len:(h + 1) * mem_len] = jnp.where(row_head == h, kt, 0.0).astype(BF16)
        vbd_ref[0, h * mem_len:(h + 1) * mem_len, :] = jnp.where(col_head == h, v, 0.0).astype(BF16)


def _layer_kernel(xp_ref, xn_ref, kbd_ref, vbd_ref, w_in_ref, conv_a_ref, sg_g_ref, sg_b_ref, sg_w_ref,
                  sg_bias_ref, pool_w_ref, pool_scale_ref, dw_w_ref, dw_b_ref, cc_g_ref, cc_b_ref,
                  pw_ref, w_out_ref, ln_g_ref, ln_b_ref, o_ref, xb_scr, hcat_scr, cx_scr, xc_scr, h_scr,
                  *proj_scrs, tile, n_seq, alpha):
    g = pl.program_id(0)
    last = pl.num_programs(0) - 1
    s = g % n_seq
    proj_of = dict(zip(PROJ_GROUPS, proj_scrs))

    @pl.when(g == 0)
    def _():
        xb0 = xp_ref[0].astype(BF16)
        for (lo, width), scr in proj_of.items():
            scr[...] = _dot(xb0, w_in_ref[:, lo:lo + width])
        hcat_scr[...] = jnp.zeros(hcat_scr.shape, BF16)

    @pl.when(s == 0)
    def _():
        cx_scr[0:HALO_A, :] = jnp.zeros((HALO_A, D_G), F32)
        xc_scr[0:HALO_C, :] = jnp.zeros((HALO_C, D_G), F32)
        h_scr[0:HALO_D, :] = jnp.zeros((HALO_D, D_G), F32)

    def out_matmuls():
        for lo in range(0, D_MODEL, 2 * D_G):
            cols = slice(lo, lo + 2 * D_G)
            o_ref[0, :, cols] = _dot(hcat_scr[...], w_out_ref[:, cols])

    def out_norm(alpha_x):
        o_ref[0] = _layer_norm(alpha_x * xp_ref[0] + o_ref[0], ln_g_ref[...], ln_b_ref[...])

    @pl.when(g < last)
    def _():
        def proj_tile(lo):
            group = [grp for grp in PROJ_GROUPS if grp[0] <= lo < grp[0] + grp[1]][0]
            return proj_of[group], lo - group[0]

        def take(lo, width):
            scr, off = proj_tile(lo)
            return scr[:, off:off + width]

        refillable = []

        def refill(*tiles):
            refillable.extend(tiles)
            while len(refillable) >= 2:
                pair = (refillable.pop(0), refillable.pop(0))
                w = jnp.concatenate([w_in_ref[:, t:t + D_G] for t in pair], axis=1)
                res = _dot(xb_scr[...], w)
                for i, t in enumerate(pair):
                    dst, dst_off = proj_tile(t)
                    dst[:, dst_off:dst_off + D_G] = res[:, i * D_G:(i + 1) * D_G]

        def gate(lo):
            return jax.nn.silu(take(OFF_GATE + lo, D_G))

        xb_scr[...] = xn_ref[0].astype(BF16)

        out_matmuls()

        pd = take(OFF_D, 2 * D_G)
        h_scr[HALO_D:HALO_D + tile, :] = pd[:, :D_G] * jax.nn.sigmoid(pd[:, D_G:])

        pa = take(OFF_A, 3 * D_G)
        xa, ba, ca = pa[:, :D_G], pa[:, D_G:2 * D_G], pa[:, 2 * D_G:]
        cx = ca * xa
        cx_scr[HALO_A:HALO_A + tile, :] = cx
        conv_a = conv_a_ref[2:3, :] * cx
        for k in range(CONV_A - 1):
            off = HALO_A - (CONV_A - 1) + k
            conv_a = conv_a + conv_a_ref[k:k + 1, :] * cx_scr[off:off + tile, :]
        h_a = (ba * conv_a * gate(0)).astype(BF16)
        cx_scr[0:HALO_A, :] = cx_scr[tile:tile + HALO_A, :]
        q = take(OFF_E, D_G).astype(BF16)

        lead = HALO_D - (CONV_D - 1)
        conv_d = []
        for c in range(tile // CONV_ROWS):
            base = c * CONV_ROWS
            bias = jnp.broadcast_to(dw_b_ref[...], (8, D_G))
            if conv_d:
                bias = _sequenced_after(bias, conv_d[-1])
            acc = jnp.concatenate([bias] * (CONV_ROWS // 8), axis=0)
            for r in range(8):
                rows = CONV_ROWS + (8 if r else 0)
                part = None
                for j in range(r, lead + CONV_D, 8):
                    if j < lead:
                        continue
                    term = dw_w_ref[j - lead:j - lead + 1, :] * h_scr[base + j - r:base + j - r + rows, :]
                    part = term if part is None else part + term
                acc = acc + part[r:r + CONV_ROWS, :]
            conv_d.append(acc)
        h_scr[0:HALO_D, :] = h_scr[tile:tile + HALO_D, :]
        refill(OFF_D, OFF_D + D_G, OFF_A, OFF_A + D_G)
        hn = jax.nn.silu(_layer_norm(jnp.concatenate(conv_d, axis=0), cc_g_ref[...], cc_b_ref[...]))
        y_d = _dot(hn.astype(BF16), pw_ref[...])
        scores = _dot(q, kbd_ref[0])
        refill(OFF_A + 2 * D_G, OFF_GATE)
        h_d = (y_d * gate(3 * D_G)).astype(BF16)

        mem_len = scores.shape[1] // N_SUB
        probs = []
        for h in range(N_SUB):
            sh = scores[:, h * mem_len:(h + 1) * mem_len]
            e = jnp.exp(sh - jnp.max(sh, axis=-1, keepdims=True))
            inv = 1.0 / jnp.sum(e, axis=-1, keepdims=True)
            probs.append((e * inv).astype(BF16))
        refill(OFF_GATE + 3 * D_G, OFF_E)
        y_e = _dot(jnp.concatenate(probs, axis=1), vbd_ref[0])

        pb = take(OFF_B, 2 * D_G)
        u = jax.nn.gelu(pb[:, :D_G])
        v = _layer_norm(jax.nn.gelu(pb[:, D_G:]), sg_g_ref[...], sg_b_ref[...])
        vb = v.astype(BF16)
        refill(OFF_B, OFF_B + D_G)
        wt = lax.broadcasted_iota(jnp.int32, (CHUNK, N_SUB * CHUNK), 0)
        ws = lax.broadcasted_iota(jnp.int32, (CHUNK, N_SUB * CHUNK), 1) % CHUNK
        w_mix = jnp.where(ws <= wt, sg_w_ref[...], 0.0).astype(BF16)
        lane_head = lax.broadcasted_iota(jnp.int32, (CHUNK, D_G), 1) // HEAD_DIM
        mixed = []
        for c in range(tile // CHUNK):
            vc = vb[c * CHUNK:(c + 1) * CHUNK, :]
            rhs = jnp.concatenate(
                [jnp.where(lane_head == h, vc, jnp.zeros_like(vc)) for h in range(N_SUB)], axis=0)
            mixed.append(_dot(w_mix, rhs) + sg_bias_ref[...])
        y_b = u * jnp.concatenate(mixed, axis=0)

        xc = take(OFF_C, D_G)
        xc_scr[HALO_C:HALO_C + tile, :] = xc

        xe = xc_scr[...]

        def doubled(val, steps):
            for shift in steps:
                val = val + pltpu.roll(val, shift, axis=0)
            return val

        s2 = doubled(xe[:, :128], (1,))
        s4 = doubled(s2, (2,))
        s8 = doubled(xe[:, 128:], (1, 2, 4))
        s16 = doubled(s8, (8,))
        lane = lax.broadcasted_iota(jnp.int32, s2.shape, 1)
        win_sum = jnp.concatenate([jnp.where(lane < HEAD_DIM, s2, s4),
                                   jnp.where(lane < HEAD_DIM, s8, s16)], axis=1)[HALO_C:, :]
        t_abs = s * tile + lax.broadcasted_iota(jnp.int32, (HALO_C, D_G), 0)
        window = jnp.left_shift(2, lax.broadcasted_iota(jnp.int32, (HALO_C, D_G), 1) // HEAD_DIM)
        count = jnp.minimum(t_abs + 1, window).astype(F32)
        inv_window = 1.0 / window[0:1, :].astype(F32)
        mean = jnp.concatenate([win_sum[:HALO_C, :] / count, win_sum[HALO_C:, :] * inv_window], axis=0)
        pooled = (mean - xc).astype(BF16)
        y_c = _dot(pooled, pool_w_ref[...]) * pool_scale_ref[...]
        xc_scr[0:HALO_C, :] = xc_scr[tile:tile + HALO_C, :]

        h_e = (y_e * gate(4 * D_G)).astype(BF16)
        refill(OFF_C, OFF_GATE + 4 * D_G)
        h_b = (y_b * gate(D_G)).astype(BF16)
        h_c = (y_c * gate(2 * D_G)).astype(BF16)
        refill(OFF_GATE + D_G, OFF_GATE + 2 * D_G)
        assert not refillable
        hcat_scr[...] = jnp.concatenate([h_a, h_b, h_c, h_d, h_e], axis=1)

        alpha8 = _sequenced_after(jnp.full((8, 128), alpha, F32), y_c[tile - CONV_ROWS:, :])
        out_norm(jnp.concatenate([jnp.concatenate([alpha8] * (D_MODEL // 128), axis=1)] * (tile // 8), axis=0))

    @pl.when(g == last)
    def _():
        out_matmuls()
        out_norm(alpha)


def _layer_spec(shape, layer):
    return pl.BlockSpec((None,) + tuple(shape[1:]), lambda g: (layer,) + (0,) * (len(shape) - 1),
                        pipeline_mode=pl.Buffered(1))


def _kv_call(mem, wkt, wv, layer):
    batch, mem_len, _ = mem.shape
    return pl.pallas_call(
        _kv_kernel,
        grid=(batch,),
        in_specs=[
            pl.BlockSpec((1, mem_len, D_MODEL), lambda b: (b, 0, 0)),
            _layer_spec(wkt.shape, layer),
            _layer_spec(wv.shape, layer),
        ],
        out_specs=[
            pl.BlockSpec((1, D_G, N_SUB * mem_len), lambda b: (b, 0, 0)),
            pl.BlockSpec((1, N_SUB * mem_len, D_G), lambda b: (b, 0, 0)),
        ],
        out_shape=[
            jax.ShapeDtypeStruct((batch, D_G, N_SUB * mem_len), BF16),
            jax.ShapeDtypeStruct((batch, N_SUB * mem_len, D_G), BF16),
        ],
        compiler_params=pltpu.CompilerParams(dimension_semantics=("arbitrary",)),
        name="mem_kv",
    )(mem, wkt, wv)


def _layer_call(x, kbd, vbd, params, layer, alpha):
    batch, seq, _ = x.shape
    tile = SEQ_TILE
    assert seq % tile == 0 and tile % CHUNK == 0 and tile % CONV_ROWS == 0
    n_seq = seq // tile
    n_tiles = batch * n_seq
    kv_width = kbd.shape[2]

    def tile_at(g):
        g = jnp.clip(g, 0, n_tiles - 1)
        return g // n_seq, g % n_seq

    in_specs = [
        pl.BlockSpec((1, tile, D_MODEL), lambda g: (*tile_at(g - 1), 0)),
        pl.BlockSpec((1, tile, D_MODEL), lambda g: (*tile_at(g + 1), 0)),
        pl.BlockSpec((1, D_G, kv_width), lambda g: (tile_at(g)[0], 0, 0)),
        pl.BlockSpec((1, kv_width, D_G), lambda g: (tile_at(g)[0], 0, 0)),
    ] + [_layer_spec(p.shape, layer) for p in params]
    return pl.pallas_call(
        functools.partial(_layer_kernel, tile=tile, n_seq=n_seq, alpha=alpha),
        grid=(n_tiles + 1,),
        in_specs=in_specs,
        out_specs=pl.BlockSpec((1, tile, D_MODEL), lambda g: (*tile_at(g - 1), 0)),
        out_shape=jax.ShapeDtypeStruct(x.shape, x.dtype),
        scratch_shapes=[
            pltpu.VMEM((tile, D_MODEL), BF16),
            pltpu.VMEM((tile, D_MIX), BF16),
            pltpu.VMEM((HALO_A + tile, D_G), F32),
            pltpu.VMEM((HALO_C + tile, D_G), F32),
            pltpu.VMEM((HALO_D + tile, D_G), F32),
        ] + [pltpu.VMEM((tile, width), F32) for _, width in PROJ_GROUPS],
        compiler_params=pltpu.CompilerParams(
            dimension_semantics=("arbitrary",),
            vmem_limit_bytes=VMEM_LIMIT_BYTES),
        name="mixer_layer",
    )(x, x, kbd, vbd, *params)


def kernel(x, mem, w_in, conv_a_w, sg_ln_g, sg_ln_b, sg_w, sg_b, pool_w, pool_scale, cc_dw_w, cc_dw_b, cc_ln_g, cc_ln_b, cc_pw_w, w_kv, w_out, ln_g, ln_b):
    depth = w_in.shape[0]
    alpha = (2.0 * depth) ** 0.25
    rows = lambda a: a.reshape(depth, 1, -1)
    wkt = w_kv[:, :, :D_G].transpose(0, 2, 1).astype(BF16)
    wv = w_kv[:, :, D_G:].astype(BF16)
    sg_w_cat = sg_w.transpose(0, 2, 1, 3).reshape(depth, CHUNK, N_SUB * CHUNK)
    sg_bias = jnp.repeat(sg_b.transpose(0, 2, 1), HEAD_DIM, axis=2)
    pool_bd = jnp.einsum("gh,lgcd->lgchd", jnp.eye(N_SUB, dtype=pool_w.dtype), pool_w)
    pool_bd = pool_bd.reshape(depth, D_G, D_G).astype(BF16)
    params = (
        w_in.astype(BF16), conv_a_w, rows(sg_ln_g), rows(sg_ln_b), sg_w_cat, sg_bias,
        pool_bd, rows(pool_scale), cc_dw_w, rows(cc_dw_b), rows(cc_ln_g), rows(cc_ln_b),
        cc_pw_w.astype(BF16), w_out.astype(BF16), rows(ln_g), rows(ln_b),
    )
    for layer in range(depth):
        kbd, vbd = _kv_call(mem, wkt, wv, layer)
        x = _layer_call(x, kbd, vbd, params, layer, alpha)
    return x
```

```python
import functools

import jax
import jax.numpy as jnp
from jax import lax
from jax.experimental import pallas as pl
from jax.experimental.pallas import tpu as pltpu

D_MODEL = 1024
D_G = D_MODEL // 4
N_SUB = 4
HEAD_DIM = D_G // N_SUB
D_MIX = 5 * D_G
D_IN = 9 * D_G + D_MIX
CHUNK = 128
CONV_A = 3
CONV_D = 31
POOL_WINDOWS = (2, 4, 8, 16)
LN_EPS = 1e-5

OFF_A = 0
OFF_B = 3 * D_G
OFF_C = 5 * D_G
OFF_D = 6 * D_G
OFF_E = 8 * D_G
OFF_GATE = 9 * D_G

HALO_A = 8
HALO_C = 16
HALO_D = 32
CONV_ROWS = 32

PROJ_GROUPS = ((OFF_A, 3 * D_G), (OFF_B, 2 * D_G), (OFF_C, D_G), (OFF_D, 2 * D_G), (OFF_E, D_G)) + tuple(
    (OFF_GATE + g * D_G, D_G) for g in range(5))

SEQ_TILE = 512
VMEM_LIMIT_BYTES = 56 * 1024 * 1024

F32 = jnp.float32
BF16 = jnp.bfloat16


def _dot(a, b):
    return jnp.dot(a, b, preferred_element_type=F32)


def _layer_norm(x, g, b):
    mu = jnp.mean(x, axis=-1, keepdims=True)
    xc = x - mu
    var = jnp.mean(xc * xc, axis=-1, keepdims=True)
    return xc * lax.rsqrt(var + LN_EPS) * g + b


def _sequenced_after(x, done):
    words = pltpu.bitcast(done, jnp.uint32)
    word = words[0:8, 0:128]
    for i in range(0, words.shape[0], 8):
        for j in range(0, words.shape[1], 128):
            if i or j:
                word = word + words[i:i + 8, j:j + 128]
    word = jnp.max(word.astype(jnp.int32), axis=-1, keepdims=True).astype(jnp.uint32)
    zero = lax.shift_right_logical(lax.shift_right_logical(word, jnp.uint32(16)), jnp.uint32(16))
    return pltpu.bitcast(pltpu.bitcast(x, jnp.uint32) | zero, F32)


def _project_memory(mem_ref, wkt_ref, wv_ref, kbd_scr, vbd_scr):
    m = mem_ref[0].astype(BF16)
    mem_len = m.shape[0]
    kt = lax.dot_general(wkt_ref[...], m, (((1,), (1,)), ((), ())),
                         preferred_element_type=F32)
    kt = kt * (1.0 / (HEAD_DIM ** 0.5))
    v = _dot(m, wv_ref[...])
    row_head = lax.broadcasted_iota(jnp.int32, kt.shape, 0) // HEAD_DIM
    col_head = lax.broadcasted_iota(jnp.int32, v.shape, 1) // HEAD_DIM
    for h in range(N_SUB):
        kbd_scr[:, h * mem_len:(h + 1) * mem_len] = jnp.where(row_head == h, kt, 0.0).astype(BF16)
        vbd_scr[h * mem_len:(h + 1) * mem_len, :] = jnp.where(col_head == h, v, 0.0).astype(BF16)


def _layer_kernel(xp_ref, xn_ref, mem_ref, wkt_ref, wv_ref, w_in_ref, conv_a_ref, sg_g_ref, sg_b_ref, sg_w_ref,
                  sg_bias_ref, pool_w_ref, pool_scale_ref, dw_w_ref, dw_b_ref, cc_g_ref, cc_b_ref,
                  pw_ref, w_out_ref, ln_g_ref, ln_b_ref, o_ref, xb_scr, hcat_scr, kbd_scr, vbd_scr, cx_scr, xc_scr, h_scr,
                  *proj_scrs, tile, n_seq, layer, alpha):
    g = pl.program_id(0)
    last = pl.num_programs(0) - 1
    s = g % n_seq
    proj_of = dict(zip(PROJ_GROUPS, proj_scrs))

    def row(ref):
        return ref[layer:layer + 1, :]

    @pl.when(g == 0)
    def _():
        xb0 = xp_ref[0].astype(BF16)
        for (lo, width), scr in proj_of.items():
            scr[...] = _dot(xb0, w_in_ref[:, lo:lo + width])
        hcat_scr[...] = jnp.zeros(hcat_scr.shape, BF16)

    @pl.when(s == 0)
    def _():
        _project_memory(mem_ref, wkt_ref, wv_ref, kbd_scr, vbd_scr)
        cx_scr[0:HALO_A, :] = jnp.zeros((HALO_A, D_G), F32)
        xc_scr[0:HALO_C, :] = jnp.zeros((HALO_C, D_G), F32)
        h_scr[0:HALO_D, :] = jnp.zeros((HALO_D, D_G), F32)

    def out_matmuls():
        for lo in range(0, D_MODEL, 2 * D_G):
            cols = slice(lo, lo + 2 * D_G)
            o_ref[0, :, cols] = _dot(hcat_scr[...], w_out_ref[:, cols])

    def out_norm(alpha_x):
        o_ref[0] = _layer_norm(alpha_x * xp_ref[0] + o_ref[0], row(ln_g_ref), row(ln_b_ref))

    @pl.when(g < last)
    def _():
        def proj_tile(lo):
            group = [grp for grp in PROJ_GROUPS if grp[0] <= lo < grp[0] + grp[1]][0]
            return proj_of[group], lo - group[0]

        def take(lo, width):
            scr, off = proj_tile(lo)
            return scr[:, off:off + width]

        refillable = []

        def refill(*tiles):
            refillable.extend(tiles)
            while len(refillable) >= 2:
                pair = (refillable.pop(0), refillable.pop(0))
                w = jnp.concatenate([w_in_ref[:, t:t + D_G] for t in pair], axis=1)
                res = _dot(xb_scr[...], w)
                for i, t in enumerate(pair):
                    dst, dst_off = proj_tile(t)
                    dst[:, dst_off:dst_off + D_G] = res[:, i * D_G:(i + 1) * D_G]

        def gate(lo):
            return jax.nn.silu(take(OFF_GATE + lo, D_G))

        xb_scr[...] = xn_ref[0].astype(BF16)

        out_matmuls()

        pd = take(OFF_D, 2 * D_G)
        h_scr[HALO_D:HALO_D + tile, :] = pd[:, :D_G] * jax.nn.sigmoid(pd[:, D_G:])

        pa = take(OFF_A, 3 * D_G)
        xa, ba, ca = pa[:, :D_G], pa[:, D_G:2 * D_G], pa[:, 2 * D_G:]
        cx = ca * xa
        cx_scr[HALO_A:HALO_A + tile, :] = cx
        conv_a = conv_a_ref[2:3, :] * cx
        for k in range(CONV_A - 1):
            off = HALO_A - (CONV_A - 1) + k
            conv_a = conv_a + conv_a_ref[k:k + 1, :] * cx_scr[off:off + tile, :]
        h_a = (ba * conv_a * gate(0)).astype(BF16)
        cx_scr[0:HALO_A, :] = cx_scr[tile:tile + HALO_A, :]
        q = take(OFF_E, D_G).astype(BF16)

        lead = HALO_D - (CONV_D - 1)
        conv_d = []
        for c in range(tile // CONV_ROWS):
            base = c * CONV_ROWS
            bias = jnp.broadcast_to(row(dw_b_ref), (8, D_G))
            if conv_d:
                bias = _sequenced_after(bias, conv_d[-1])
            acc = jnp.concatenate([bias] * (CONV_ROWS // 8), axis=0)
            for r in range(8):
                rows = CONV_ROWS + (8 if r else 0)
                part = None
                for j in range(r, lead + CONV_D, 8):
                    if j < lead:
                        continue
                    term = dw_w_ref[j - lead:j - lead + 1, :] * h_scr[base + j - r:base + j - r + rows, :]
                    part = term if part is None else part + term
                acc = acc + part[r:r + CONV_ROWS, :]
            conv_d.append(acc)
        h_scr[0:HALO_D, :] = h_scr[tile:tile + HALO_D, :]
        refill(OFF_D, OFF_D + D_G, OFF_A, OFF_A + D_G)
        hn = jax.nn.silu(_layer_norm(jnp.concatenate(conv_d, axis=0), row(cc_g_ref), row(cc_b_ref)))
        y_d = _dot(hn.astype(BF16), pw_ref[...])
        scores = _dot(q, kbd_scr[...])
        refill(OFF_A + 2 * D_G, OFF_GATE)
        h_d = (y_d * gate(3 * D_G)).astype(BF16)

        mem_len = scores.shape[1] // N_SUB
        probs = []
        for h in range(N_SUB):
            sh = scores[:, h * mem_len:(h + 1) * mem_len]
            e = jnp.exp(sh - jnp.max(sh, axis=-1, keepdims=True))
            inv = 1.0 / jnp.sum(e, axis=-1, keepdims=True)
            probs.append((e * inv).astype(BF16))
        refill(OFF_GATE + 3 * D_G, OFF_E)
        y_e = _dot(jnp.concatenate(probs, axis=1), vbd_scr[...])

        pb = take(OFF_B, 2 * D_G)
        u = jax.nn.gelu(pb[:, :D_G])
        v = _layer_norm(jax.nn.gelu(pb[:, D_G:]), row(sg_g_ref), row(sg_b_ref))
        vb = v.astype(BF16)
        refill(OFF_B, OFF_B + D_G)
        wt = lax.broadcasted_iota(jnp.int32, (CHUNK, N_SUB * CHUNK), 0)
        ws = lax.broadcasted_iota(jnp.int32, (CHUNK, N_SUB * CHUNK), 1) % CHUNK
        w_mix = jnp.where(ws <= wt, sg_w_ref[...], 0.0).astype(BF16)
        lane_head = lax.broadcasted_iota(jnp.int32, (CHUNK, D_G), 1) // HEAD_DIM
        mixed = []
        for c in range(tile // CHUNK):
            vc = vb[c * CHUNK:(c + 1) * CHUNK, :]
            rhs = jnp.concatenate(
                [jnp.where(lane_head == h, vc, jnp.zeros_like(vc)) for h in range(N_SUB)], axis=0)
            mixed.append(_dot(w_mix, rhs) + sg_bias_ref[...])
        y_b = u * jnp.concatenate(mixed, axis=0)

        xc = take(OFF_C, D_G)
        xc_scr[HALO_C:HALO_C + tile, :] = xc

        xe = xc_scr[...]

        def doubled(val, steps):
            for shift in steps:
                val = val + pltpu.roll(val, shift, axis=0)
            return val

        s2 = doubled(xe[:, :128], (1,))
        s4 = doubled(s2, (2,))
        s8 = doubled(xe[:, 128:], (1, 2, 4))
        s16 = doubled(s8, (8,))
        lane = lax.broadcasted_iota(jnp.int32, s2.shape, 1)
        win_sum = jnp.concatenate([jnp.where(lane < HEAD_DIM, s2, s4),
                                   jnp.where(lane < HEAD_DIM, s8, s16)], axis=1)[HALO_C:, :]
        t_abs = s * tile + lax.broadcasted_iota(jnp.int32, (HALO_C, D_G), 0)
        window = jnp.left_shift(2, lax.broadcasted_iota(jnp.int32, (HALO_C, D_G), 1) // HEAD_DIM)
        count = jnp.minimum(t_abs + 1, window).astype(F32)
        inv_window = 1.0 / window[0:1, :].astype(F32)
        mean = jnp.concatenate([win_sum[:HALO_C, :] / count, win_sum[HALO_C:, :] * inv_window], axis=0)
        pooled = (mean - xc).astype(BF16)
        y_c = _dot(pooled, pool_w_ref[...]) * row(pool_scale_ref)
        xc_scr[0:HALO_C, :] = xc_scr[tile:tile + HALO_C, :]

        h_e = (y_e * gate(4 * D_G)).astype(BF16)
        refill(OFF_C, OFF_GATE + 4 * D_G)
        h_b = (y_b * gate(D_G)).astype(BF16)
        h_c = (y_c * gate(2 * D_G)).astype(BF16)
        refill(OFF_GATE + D_G, OFF_GATE + 2 * D_G)
        assert not refillable
        hcat_scr[...] = jnp.concatenate([h_a, h_b, h_c, h_d, h_e], axis=1)

        alpha8 = _sequenced_after(jnp.full((8, 128), alpha, F32), y_c[tile - CONV_ROWS:, :])
        out_norm(jnp.concatenate([jnp.concatenate([alpha8] * (D_MODEL // 128), axis=1)] * (tile // 8), axis=0))

    @pl.when(g == last)
    def _():
        out_matmuls()
        out_norm(alpha)


def _layer_spec(shape, layer):
    resident = dict(pipeline_mode=pl.Buffered(1))
    if len(shape) == 2:
        return pl.BlockSpec(tuple(shape), lambda g: (0, 0), **resident)
    return pl.BlockSpec((None,) + tuple(shape[1:]), lambda g: (layer,) + (0,) * (len(shape) - 1), **resident)


def _layer_call(x, mem, params, layer, alpha):
    batch, seq, _ = x.shape
    tile = SEQ_TILE
    assert seq % tile == 0 and tile % CHUNK == 0 and tile % CONV_ROWS == 0
    n_seq = seq // tile
    n_tiles = batch * n_seq
    mem_len = mem.shape[1]

    def tile_at(g):
        g = jnp.clip(g, 0, n_tiles - 1)
        return g // n_seq, g % n_seq

    in_specs = [
        pl.BlockSpec((1, tile, D_MODEL), lambda g: (*tile_at(g - 1), 0)),
        pl.BlockSpec((1, tile, D_MODEL), lambda g: (*tile_at(g + 1), 0)),
        pl.BlockSpec((1, mem_len, D_MODEL), lambda g: (tile_at(g)[0], 0, 0)),
    ] + [_layer_spec(p.shape, layer) for p in params]
    return pl.pallas_call(
        functools.partial(_layer_kernel, tile=tile, n_seq=n_seq, layer=layer, alpha=alpha),
        grid=(n_tiles + 1,),
        in_specs=in_specs,
        out_specs=pl.BlockSpec((1, tile, D_MODEL), lambda g: (*tile_at(g - 1), 0)),
        out_shape=jax.ShapeDtypeStruct(x.shape, x.dtype),
        scratch_shapes=[
            pltpu.VMEM((tile, D_MODEL), BF16),
            pltpu.VMEM((tile, D_MIX), BF16),
            pltpu.VMEM((D_G, N_SUB * mem_len), BF16),
            pltpu.VMEM((N_SUB * mem_len, D_G), BF16),
            pltpu.VMEM((HALO_A + tile, D_G), F32),
            pltpu.VMEM((HALO_C + tile, D_G), F32),
            pltpu.VMEM((HALO_D + tile, D_G), F32),
        ] + [pltpu.VMEM((tile, width), F32) for _, width in PROJ_GROUPS],
        compiler_params=pltpu.CompilerParams(
            dimension_semantics=("arbitrary",),
            vmem_limit_bytes=VMEM_LIMIT_BYTES),
        name="mixer_layer",
    )(x, x, mem, *params)


def kernel(x, mem, w_in, conv_a_w, sg_ln_g, sg_ln_b, sg_w, sg_b, pool_w, pool_scale, cc_dw_w, cc_dw_b, cc_ln_g, cc_ln_b, cc_pw_w, w_kv, w_out, ln_g, ln_b):
    depth = w_in.shape[0]
    alpha = (2.0 * depth) ** 0.25
    wkt = w_kv[:, :, :D_G].transpose(0, 2, 1).astype(BF16)
    wv = w_kv[:, :, D_G:].astype(BF16)
    sg_w_cat = sg_w.transpose(0, 2, 1, 3).reshape(depth, CHUNK, N_SUB * CHUNK)
    sg_bias = jnp.repeat(sg_b.transpose(0, 2, 1), HEAD_DIM, axis=2)
    pool_bd = jnp.einsum("gh,lgcd->lgchd", jnp.eye(N_SUB, dtype=pool_w.dtype), pool_w)
    pool_bd = pool_bd.reshape(depth, D_G, D_G).astype(BF16)
    params = (
        wkt, wv, w_in.astype(BF16), conv_a_w, sg_ln_g, sg_ln_b, sg_w_cat, sg_bias,
        pool_bd, pool_scale, cc_dw_w, cc_dw_b, cc_ln_g, cc_ln_b,
        cc_pw_w.astype(BF16), w_out.astype(BF16), ln_g, ln_b,
    )
    for layer in range(depth):
        x = _layer_call(x, mem, params, layer, alpha)
    return x
```

```python
import functools

import jax
import jax.numpy as jnp
from jax import lax
from jax.experimental import pallas as pl
from jax.experimental.pallas import tpu as pltpu

D_MODEL = 1024
D_G = D_MODEL // 4
N_SUB = 4
HEAD_DIM = D_G // N_SUB
D_MIX = 5 * D_G
D_IN = 9 * D_G + D_MIX
CHUNK = 128
CONV_A = 3
CONV_D = 31
POOL_WINDOWS = (2, 4, 8, 16)
LN_EPS = 1e-5

OFF_A = 0
OFF_B = 3 * D_G
OFF_C = 5 * D_G
OFF_D = 6 * D_G
OFF_E = 8 * D_G
OFF_GATE = 9 * D_G

HALO_A = 8
HALO_C = 16
HALO_D = 32
CONV_ROWS = 32
CONV_CHAINS = 2
OUT_NORM_CHUNKS = 4
OUT_NORM_FIRST_REFILL = 2

PROJ_GROUPS = ((OFF_A, 3 * D_G), (OFF_B, 2 * D_G), (OFF_C, D_G), (OFF_D, 2 * D_G), (OFF_E, D_G)) + tuple(
    (OFF_GATE + g * D_G, D_G) for g in range(5))

SEQ_TILE = 512
VMEM_LIMIT_BYTES = 56 * 1024 * 1024

F32 = jnp.float32
BF16 = jnp.bfloat16


def _dot(a, b):
    return jnp.dot(a, b, preferred_element_type=F32)


def _layer_norm(x, g, b):
    mu = jnp.mean(x, axis=-1, keepdims=True)
    xc = x - mu
    var = jnp.mean(xc * xc, axis=-1, keepdims=True)
    return xc * lax.rsqrt(var + LN_EPS) * g + b


def _sequenced_after(x, done):
    words = pltpu.bitcast(done, jnp.uint32)
    word = words[0:8, 0:128]
    for i in range(0, words.shape[0], 8):
        for j in range(0, words.shape[1], 128):
            if i or j:
                word = word + words[i:i + 8, j:j + 128]
    word = jnp.max(word.astype(jnp.int32), axis=-1, keepdims=True).astype(jnp.uint32)
    zero = lax.shift_right_logical(lax.shift_right_logical(word, jnp.uint32(16)), jnp.uint32(16))
    return pltpu.bitcast(pltpu.bitcast(x, jnp.uint32) | zero, F32)


def _project_memory(mem_ref, wkt_ref, wv_ref, kbd_scr, vbd_scr):
    m = mem_ref[0].astype(BF16)
    mem_len = m.shape[0]
    kt = lax.dot_general(wkt_ref[...], m, (((1,), (1,)), ((), ())),
                         preferred_element_type=F32)
    kt = kt * (1.0 / (HEAD_DIM ** 0.5))
    v = _dot(m, wv_ref[...])
    row_head = lax.broadcasted_iota(jnp.int32, kt.shape, 0) // HEAD_DIM
    col_head = lax.broadcasted_iota(jnp.int32, v.shape, 1) // HEAD_DIM
    for h in range(N_SUB):
        kbd_scr[:, h * mem_len:(h + 1) * mem_len] = jnp.where(row_head == h, kt, 0.0).astype(BF16)
        vbd_scr[h * mem_len:(h + 1) * mem_len, :] = jnp.where(col_head == h, v, 0.0).astype(BF16)


def _layer_kernel(xp_ref, xn_ref, mem_ref, wkt_ref, wv_ref, w_in_ref, conv_a_ref, sg_g_ref, sg_b_ref, sg_w_ref,
                  sg_bias_ref, pool_w_ref, pool_scale_ref, dw_w_ref, dw_b_ref, cc_g_ref, cc_b_ref,
                  pw_ref, w_out_ref, ln_g_ref, ln_b_ref, o_ref, xb_scr, hcat_scr, kbd_scr, vbd_scr, cx_scr, xc_scr, h_scr,
                  *proj_scrs, tile, n_seq, layer, alpha):
    g = pl.program_id(0)
    last = pl.num_programs(0) - 1
    s = g % n_seq
    proj_of = dict(zip(PROJ_GROUPS, proj_scrs))

    def row(ref):
        return ref[layer:layer + 1, :]

    @pl.when(g == 0)
    def _():
        xb0 = xp_ref[0].astype(BF16)
        for (lo, width), scr in proj_of.items():
            scr[...] = _dot(xb0, w_in_ref[:, lo:lo + width])
        hcat_scr[...] = jnp.zeros(hcat_scr.shape, BF16)

    @pl.when(s == 0)
    def _():
        _project_memory(mem_ref, wkt_ref, wv_ref, kbd_scr, vbd_scr)
        cx_scr[0:HALO_A, :] = jnp.zeros((HALO_A, D_G), F32)
        xc_scr[0:HALO_C, :] = jnp.zeros((HALO_C, D_G), F32)
        h_scr[0:HALO_D, :] = jnp.zeros((HALO_D, D_G), F32)

    def out_matmuls():
        for lo in range(0, D_MODEL, 2 * D_G):
            cols = slice(lo, lo + 2 * D_G)
            o_ref[0, :, cols] = _dot(hcat_scr[...], w_out_ref[:, cols])

    def out_norm(alpha_x, rows=slice(None)):
        o_ref[0, rows, :] = _layer_norm(alpha_x * xp_ref[0, rows, :] + o_ref[0, rows, :],
                                        row(ln_g_ref), row(ln_b_ref))

    @pl.when(g < last)
    def _():
        def proj_tile(lo):
            group = [grp for grp in PROJ_GROUPS if grp[0] <= lo < grp[0] + grp[1]][0]
            return proj_of[group], lo - group[0]

        def take(lo, width):
            scr, off = proj_tile(lo)
            return scr[:, off:off + width]

        refillable = []
        refill_done = []

        def refill(*tiles):
            refillable.extend(tiles)
            while len(refillable) >= 2:
                pair = (refillable.pop(0), refillable.pop(0))
                w = jnp.concatenate([w_in_ref[:, t:t + D_G] for t in pair], axis=1)
                res = _dot(xb_scr[...], w)
                for i, t in enumerate(pair):
                    dst, dst_off = proj_tile(t)
                    dst[:, dst_off:dst_off + D_G] = res[:, i * D_G:(i + 1) * D_G]
                refill_done.append(res[tile - 8:, 2 * D_G - 128:])

        def gate(lo):
            return jax.nn.silu(take(OFF_GATE + lo, D_G))

        xb_scr[...] = xn_ref[0].astype(BF16)

        out_matmuls()

        pd = take(OFF_D, 2 * D_G)
        h_scr[HALO_D:HALO_D + tile, :] = pd[:, :D_G] * jax.nn.sigmoid(pd[:, D_G:])

        pa = take(OFF_A, 3 * D_G)
        xa, ba, ca = pa[:, :D_G], pa[:, D_G:2 * D_G], pa[:, 2 * D_G:]
        cx = ca * xa
        cx_scr[HALO_A:HALO_A + tile, :] = cx
        conv_a = conv_a_ref[2:3, :] * cx
        for k in range(CONV_A - 1):
            off = HALO_A - (CONV_A - 1) + k
            conv_a = conv_a + conv_a_ref[k:k + 1, :] * cx_scr[off:off + tile, :]
        h_a = (ba * conv_a * gate(0)).astype(BF16)
        cx_scr[0:HALO_A, :] = cx_scr[tile:tile + HALO_A, :]
        q = take(OFF_E, D_G).astype(BF16)

        lead = HALO_D - (CONV_D - 1)
        conv_d = []
        for c in range(tile // CONV_ROWS):
            base = c * CONV_ROWS
            bias = jnp.broadcast_to(row(dw_b_ref), (8, D_G))
            if len(conv_d) >= CONV_CHAINS:
                bias = _sequenced_after(bias, conv_d[-CONV_CHAINS])
            acc = jnp.concatenate([bias] * (CONV_ROWS // 8), axis=0)
            for r in range(8):
                rows = CONV_ROWS + (8 if r else 0)
                part = None
                for j in range(r, lead + CONV_D, 8):
                    if j < lead:
                        continue
                    term = dw_w_ref[j - lead:j - lead + 1, :] * h_scr[base + j - r:base + j - r + rows, :]
                    part = term if part is None else part + term
                acc = acc + part[r:r + CONV_ROWS, :]
            conv_d.append(acc)
        h_scr[0:HALO_D, :] = h_scr[tile:tile + HALO_D, :]
        refill(OFF_D, OFF_D + D_G, OFF_A, OFF_A + D_G)
        hn = jax.nn.silu(_layer_norm(jnp.concatenate(conv_d, axis=0), row(cc_g_ref), row(cc_b_ref)))
        y_d = _dot(hn.astype(BF16), pw_ref[...])
        scores = _dot(q, kbd_scr[...])
        refill(OFF_A + 2 * D_G, OFF_GATE)
        h_d = (y_d * gate(3 * D_G)).astype(BF16)

        mem_len = scores.shape[1] // N_SUB
        probs = []
        for h in range(N_SUB):
            sh = scores[:, h * mem_len:(h + 1) * mem_len]
            e = jnp.exp(sh - jnp.max(sh, axis=-1, keepdims=True))
            inv = 1.0 / jnp.sum(e, axis=-1, keepdims=True)
            probs.append((e * inv).astype(BF16))
        refill(OFF_GATE + 3 * D_G, OFF_E)
        y_e = _dot(jnp.concatenate(probs, axis=1), vbd_scr[...])

        pb = take(OFF_B, 2 * D_G)
        u = jax.nn.gelu(pb[:, :D_G])
        v = _layer_norm(jax.nn.gelu(pb[:, D_G:]), row(sg_g_ref), row(sg_b_ref))
        vb = v.astype(BF16)
        refill(OFF_B, OFF_B + D_G)
        wt = lax.broadcasted_iota(jnp.int32, (CHUNK, N_SUB * CHUNK), 0)
        ws = lax.broadcasted_iota(jnp.int32, (CHUNK, N_SUB * CHUNK), 1) % CHUNK
        w_mix = jnp.where(ws <= wt, sg_w_ref[...], 0.0).astype(BF16)
        lane_head = lax.broadcasted_iota(jnp.int32, (CHUNK, D_G), 1) // HEAD_DIM
        mixed = []
        for c in range(tile // CHUNK):
            vc = vb[c * CHUNK:(c + 1) * CHUNK, :]
            rhs = jnp.concatenate(
                [jnp.where(lane_head == h, vc, jnp.zeros_like(vc)) for h in range(N_SUB)], axis=0)
            mixed.append(_dot(w_mix, rhs) + sg_bias_ref[...])
        y_b = u * jnp.concatenate(mixed, axis=0)

        xc = take(OFF_C, D_G)
        xc_scr[HALO_C:HALO_C + tile, :] = xc

        xe = xc_scr[...]

        def doubled(val, steps):
            for shift in steps:
                val = val + pltpu.roll(val, shift, axis=0)
            return val

        s2 = doubled(xe[:, :128], (1,))
        s4 = doubled(s2, (2,))
        s8 = doubled(xe[:, 128:], (1, 2, 4))
        s16 = doubled(s8, (8,))
        lane = lax.broadcasted_iota(jnp.int32, s2.shape, 1)
        win_sum = jnp.concatenate([jnp.where(lane < HEAD_DIM, s2, s4),
                                   jnp.where(lane < HEAD_DIM, s8, s16)], axis=1)[HALO_C:, :]
        t_abs = s * tile + lax.broadcasted_iota(jnp.int32, (HALO_C, D_G), 0)
        window = jnp.left_shift(2, lax.broadcasted_iota(jnp.int32, (HALO_C, D_G), 1) // HEAD_DIM)
        count = jnp.minimum(t_abs + 1, window).astype(F32)
        inv_window = 1.0 / window[0:1, :].astype(F32)
        mean = jnp.concatenate([win_sum[:HALO_C, :] / count, win_sum[HALO_C:, :] * inv_window], axis=0)
        pooled = (mean - xc).astype(BF16)
        y_c = _dot(pooled, pool_w_ref[...]) * row(pool_scale_ref)
        xc_scr[0:HALO_C, :] = xc_scr[tile:tile + HALO_C, :]

        h_e = (y_e * gate(4 * D_G)).astype(BF16)
        refill(OFF_C, OFF_GATE + 4 * D_G)
        h_b = (y_b * gate(D_G)).astype(BF16)
        h_c = (y_c * gate(2 * D_G)).astype(BF16)
        refill(OFF_GATE + D_G, OFF_GATE + 2 * D_G)
        assert not refillable
        hcat_scr[...] = jnp.concatenate([h_a, h_b, h_c, h_d, h_e], axis=1)

        rows = tile // OUT_NORM_CHUNKS
        for c in range(OUT_NORM_CHUNKS):
            alpha8 = _sequenced_after(jnp.full((8, 128), alpha, F32), refill_done[OUT_NORM_FIRST_REFILL + c])
            alpha_t = jnp.concatenate([jnp.concatenate([alpha8] * (D_MODEL // 128), axis=1)] * (rows // 8), axis=0)
            out_norm(alpha_t, slice(c * rows, (c + 1) * rows))

    @pl.when(g == last)
    def _():
        out_matmuls()
        out_norm(alpha)


def _layer_spec(shape, layer):
    resident = dict(pipeline_mode=pl.Buffered(1))
    if len(shape) == 2:
        return pl.BlockSpec(tuple(shape), lambda g: (0, 0), **resident)
    return pl.BlockSpec((None,) + tuple(shape[1:]), lambda g: (layer,) + (0,) * (len(shape) - 1), **resident)


def _layer_call(x, mem, params, layer, alpha):
    batch, seq, _ = x.shape
    tile = SEQ_TILE
    assert seq % tile == 0 and tile % CHUNK == 0 and tile % CONV_ROWS == 0
    n_seq = seq // tile
    n_tiles = batch * n_seq
    mem_len = mem.shape[1]

    def tile_at(g):
        g = jnp.clip(g, 0, n_tiles - 1)
        return g // n_seq, g % n_seq

    in_specs = [
        pl.BlockSpec((1, tile, D_MODEL), lambda g: (*tile_at(g - 1), 0)),
        pl.BlockSpec((1, tile, D_MODEL), lambda g: (*tile_at(g + 1), 0)),
        pl.BlockSpec((1, mem_len, D_MODEL), lambda g: (tile_at(g)[0], 0, 0)),
    ] + [_layer_spec(p.shape, layer) for p in params]
    return pl.pallas_call(
        functools.partial(_layer_kernel, tile=tile, n_seq=n_seq, layer=layer, alpha=alpha),
        grid=(n_tiles + 1,),
        in_specs=in_specs,
        out_specs=pl.BlockSpec((1, tile, D_MODEL), lambda g: (*tile_at(g - 1), 0)),
        out_shape=jax.ShapeDtypeStruct(x.shape, x.dtype),
        scratch_shapes=[
            pltpu.VMEM((tile, D_MODEL), BF16),
            pltpu.VMEM((tile, D_MIX), BF16),
            pltpu.VMEM((D_G, N_SUB * mem_len), BF16),
            pltpu.VMEM((N_SUB * mem_len, D_G), BF16),
            pltpu.VMEM((HALO_A + tile, D_G), F32),
            pltpu.VMEM((HALO_C + tile, D_G), F32),
            pltpu.VMEM((HALO_D + tile, D_G), F32),
        ] + [pltpu.VMEM((tile, width), F32) for _, width in PROJ_GROUPS],
        compiler_params=pltpu.CompilerParams(
            dimension_semantics=("arbitrary",),
            vmem_limit_bytes=VMEM_LIMIT_BYTES),
        name="mixer_layer",
    )(x, x, mem, *params)


def kernel(x, mem, w_in, conv_a_w, sg_ln_g, sg_ln_b, sg_w, sg_b, pool_w, pool_scale, cc_dw_w, cc_dw_b, cc_ln_g, cc_ln_b, cc_pw_w, w_kv, w_out, ln_g, ln_b):
    depth = w_in.shape[0]
    alpha = (2.0 * depth) ** 0.25
    wkt = w_kv[:, :, :D_G].transpose(0, 2, 1).astype(BF16)
    wv = w_kv[:, :, D_G:].astype(BF16)
    sg_w_cat = sg_w.transpose(0, 2, 1, 3).reshape(depth, CHUNK, N_SUB * CHUNK)
    sg_bias = jnp.repeat(sg_b.transpose(0, 2, 1), HEAD_DIM, axis=2)
    pool_bd = jnp.einsum("gh,lgcd->lgchd", jnp.eye(N_SUB, dtype=pool_w.dtype), pool_w)
    pool_bd = pool_bd.reshape(depth, D_G, D_G).astype(BF16)
    params = (
        wkt, wv, w_in.astype(BF16), conv_a_w, sg_ln_g, sg_ln_b, sg_w_cat, sg_bias,
        pool_bd, pool_scale, cc_dw_w, cc_dw_b, cc_ln_g, cc_ln_b,
        cc_pw_w.astype(BF16), w_out.astype(BF16), ln_g, ln_b,
    )
    for layer in range(depth):
        x = _layer_call(x, mem, params, layer, alpha)
    return x
```

```python
import functools

import jax
import jax.numpy as jnp
from jax import lax
from jax.experimental import pallas as pl
from jax.experimental.pallas import tpu as pltpu

D_MODEL = 1024
D_G = D_MODEL // 4
N_SUB = 4
HEAD_DIM = D_G // N_SUB
D_MIX = 5 * D_G
D_IN = 9 * D_G + D_MIX
CHUNK = 128
CONV_A = 3
CONV_D = 31
POOL_WINDOWS = (2, 4, 8, 16)
LN_EPS = 1e-5

OFF_A = 0
OFF_B = 3 * D_G
OFF_C = 5 * D_G
OFF_D = 6 * D_G
OFF_E = 8 * D_G
OFF_GATE = 9 * D_G

HALO_A = 8
HALO_C = 16
HALO_D = 32
CONV_ROWS = 32
CONV_CHAINS = 2
OUT_NORM_CHUNKS = 4
OUT_NORM_FIRST_REFILL = 2

PROJ_GROUPS = ((OFF_A, 3 * D_G), (OFF_B, 2 * D_G), (OFF_C, D_G), (OFF_D, 2 * D_G), (OFF_E, D_G)) + tuple(
    (OFF_GATE + g * D_G, D_G) for g in range(5))

SEQ_TILE = 512
VMEM_LIMIT_BYTES = 56 * 1024 * 1024

F32 = jnp.float32
BF16 = jnp.bfloat16


def _dot(a, b):
    return jnp.dot(a, b, preferred_element_type=F32)


def _layer_norm(x, g, b):
    mu = jnp.mean(x, axis=-1, keepdims=True)
    xc = x - mu
    var = jnp.mean(xc * xc, axis=-1, keepdims=True)
    return xc * lax.rsqrt(var + LN_EPS) * g + b


def _sequenced_after(x, done):
    words = pltpu.bitcast(done, jnp.uint32)
    word = words[0:8, 0:128]
    for i in range(0, words.shape[0], 8):
        for j in range(0, words.shape[1], 128):
            if i or j:
                word = word + words[i:i + 8, j:j + 128]
    word = jnp.max(word.astype(jnp.int32), axis=-1, keepdims=True).astype(jnp.uint32)
    zero = lax.shift_right_logical(lax.shift_right_logical(word, jnp.uint32(16)), jnp.uint32(16))
    return pltpu.bitcast(pltpu.bitcast(x, jnp.uint32) | zero, F32)


def _project_memory(mem_ref, wkt_ref, wv_ref, kbd_scr, vbd_scr):
    m = mem_ref[0].astype(BF16)
    mem_len = m.shape[0]
    kt = lax.dot_general(wkt_ref[...], m, (((1,), (1,)), ((), ())),
                         preferred_element_type=F32)
    kt = kt * (1.0 / (HEAD_DIM ** 0.5))
    v = _dot(m, wv_ref[...])
    row_head = lax.broadcasted_iota(jnp.int32, kt.shape, 0) // HEAD_DIM
    col_head = lax.broadcasted_iota(jnp.int32, v.shape, 1) // HEAD_DIM
    for h in range(N_SUB):
        kbd_scr[:, h * mem_len:(h + 1) * mem_len] = jnp.where(row_head == h, kt, 0.0).astype(BF16)
        vbd_scr[h * mem_len:(h + 1) * mem_len, :] = jnp.where(col_head == h, v, 0.0).astype(BF16)


def _layer_kernel(xp_ref, xn_ref, mem_ref, wkt_ref, wv_ref, w_in_ref, conv_a_ref, sg_g_ref, sg_b_ref, sg_w_ref,
                  sg_bias_ref, pool_w_ref, pool_scale_ref, dw_w_ref, dw_b_ref, cc_g_ref, cc_b_ref,
                  pw_ref, w_out_ref, ln_g_ref, ln_b_ref, o_ref, xb_scr, hcat_scr, kbd_scr, vbd_scr, cx_scr, xc_scr, h_scr,
                  *proj_scrs, tile, n_seq, layer, alpha):
    g = pl.program_id(0)
    last = pl.num_programs(0) - 1
    s = g % n_seq
    proj_of = dict(zip(PROJ_GROUPS, proj_scrs))

    def row(ref):
        return ref[layer:layer + 1, :]

    @pl.when(g == 0)
    def _():
        xb0 = xp_ref[0].astype(BF16)
        for (lo, width), scr in proj_of.items():
            scr[...] = _dot(xb0, w_in_ref[:, lo:lo + width])
        hcat_scr[...] = jnp.zeros(hcat_scr.shape, BF16)

    @pl.when(s == 0)
    def _():
        _project_memory(mem_ref, wkt_ref, wv_ref, kbd_scr, vbd_scr)
        cx_scr[0:HALO_A, :] = jnp.zeros((HALO_A, D_G), F32)
        xc_scr[0:HALO_C, :] = jnp.zeros((HALO_C, D_G), F32)
        h_scr[0:HALO_D, :] = jnp.zeros((HALO_D, D_G), F32)

    def out_matmul():
        o_ref[0] = _dot(hcat_scr[...], w_out_ref[...])

    def out_norm(alpha_x, rows=slice(None)):
        o_ref[0, rows, :] = _layer_norm(alpha_x * xp_ref[0, rows, :] + o_ref[0, rows, :],
                                        row(ln_g_ref), row(ln_b_ref))

    @pl.when(g < last)
    def _():
        def proj_tile(lo):
            group = [grp for grp in PROJ_GROUPS if grp[0] <= lo < grp[0] + grp[1]][0]
            return proj_of[group], lo - group[0]

        def take(lo, width):
            scr, off = proj_tile(lo)
            return scr[:, off:off + width]

        refillable = []
        refill_done = []

        def refill(*tiles):
            refillable.extend(tiles)
            while len(refillable) >= 2:
                pair = (refillable.pop(0), refillable.pop(0))
                w = jnp.concatenate([w_in_ref[:, t:t + D_G] for t in pair], axis=1)
                res = _dot(xb_scr[...], w)
                for i, t in enumerate(pair):
                    dst, dst_off = proj_tile(t)
                    dst[:, dst_off:dst_off + D_G] = res[:, i * D_G:(i + 1) * D_G]
                refill_done.append(res[tile - 8:, 2 * D_G - 128:])

        def gate(lo):
            return jax.nn.silu(take(OFF_GATE + lo, D_G))

        xb_scr[...] = xn_ref[0].astype(BF16)

        out_matmul()

        pd = take(OFF_D, 2 * D_G)
        h_scr[HALO_D:HALO_D + tile, :] = pd[:, :D_G] * jax.nn.sigmoid(pd[:, D_G:])

        pa = take(OFF_A, 3 * D_G)
        xa, ba, ca = pa[:, :D_G], pa[:, D_G:2 * D_G], pa[:, 2 * D_G:]
        cx = ca * xa
        cx_scr[HALO_A:HALO_A + tile, :] = cx
        conv_a = conv_a_ref[2:3, :] * cx
        for k in range(CONV_A - 1):
            off = HALO_A - (CONV_A - 1) + k
            conv_a = conv_a + conv_a_ref[k:k + 1, :] * cx_scr[off:off + tile, :]
        h_a = (ba * conv_a * gate(0)).astype(BF16)
        cx_scr[0:HALO_A, :] = cx_scr[tile:tile + HALO_A, :]
        q = take(OFF_E, D_G).astype(BF16)

        lead = HALO_D - (CONV_D - 1)
        conv_d = []
        for c in range(tile // CONV_ROWS):
            base = c * CONV_ROWS
            bias = jnp.broadcast_to(row(dw_b_ref), (8, D_G))
            if len(conv_d) >= CONV_CHAINS:
                bias = _sequenced_after(bias, conv_d[-CONV_CHAINS])
            acc = jnp.concatenate([bias] * (CONV_ROWS // 8), axis=0)
            for r in range(8):
                rows = CONV_ROWS + (8 if r else 0)
                part = None
                for j in range(r, lead + CONV_D, 8):
                    if j < lead:
                        continue
                    term = dw_w_ref[j - lead:j - lead + 1, :] * h_scr[base + j - r:base + j - r + rows, :]
                    part = term if part is None else part + term
                acc = acc + part[r:r + CONV_ROWS, :]
            conv_d.append(acc)
        h_scr[0:HALO_D, :] = h_scr[tile:tile + HALO_D, :]
        refill(OFF_D, OFF_D + D_G, OFF_A, OFF_A + D_G)
        hn = jax.nn.silu(_layer_norm(jnp.concatenate(conv_d, axis=0), row(cc_g_ref), row(cc_b_ref)))
        y_d = _dot(hn.astype(BF16), pw_ref[...])
        scores = _dot(q, kbd_scr[...])
        refill(OFF_A + 2 * D_G, OFF_GATE)
        h_d = (y_d * gate(3 * D_G)).astype(BF16)

        mem_len = scores.shape[1] // N_SUB
        probs = []
        for h in range(N_SUB):
            sh = scores[:, h * mem_len:(h + 1) * mem_len]
            e = jnp.exp(sh - jnp.max(sh, axis=-1, keepdims=True))
            inv = 1.0 / jnp.sum(e, axis=-1, keepdims=True)
            probs.append((e * inv).astype(BF16))
        refill(OFF_GATE + 3 * D_G, OFF_E)
        y_e = _dot(jnp.concatenate(probs, axis=1), vbd_scr[...])

        pb = take(OFF_B, 2 * D_G)
        u = jax.nn.gelu(pb[:, :D_G])
        v = _layer_norm(jax.nn.gelu(pb[:, D_G:]), row(sg_g_ref), row(sg_b_ref))
        vb = v.astype(BF16)
        refill(OFF_B, OFF_B + D_G)
        wt = lax.broadcasted_iota(jnp.int32, (CHUNK, N_SUB * CHUNK), 0)
        ws = lax.broadcasted_iota(jnp.int32, (CHUNK, N_SUB * CHUNK), 1) % CHUNK
        w_mix = jnp.where(ws <= wt, sg_w_ref[...], 0.0).astype(BF16)
        lane_head = lax.broadcasted_iota(jnp.int32, (CHUNK, D_G), 1) // HEAD_DIM
        mixed = []
        for c in range(tile // CHUNK):
            vc = vb[c * CHUNK:(c + 1) * CHUNK, :]
            rhs = jnp.concatenate(
                [jnp.where(lane_head == h, vc, jnp.zeros_like(vc)) for h in range(N_SUB)], axis=0)
            mixed.append(_dot(w_mix, rhs) + sg_bias_ref[...])
        y_b = u * jnp.concatenate(mixed, axis=0)

        xc = take(OFF_C, D_G)
        xc_scr[HALO_C:HALO_C + tile, :] = xc

        xe = xc_scr[...]

        def doubled(val, steps):
            for shift in steps:
                val = val + pltpu.roll(val, shift, axis=0)
            return val

        s2 = doubled(xe[:, :128], (1,))
        s4 = doubled(s2, (2,))
        s8 = doubled(xe[:, 128:], (1, 2, 4))
        s16 = doubled(s8, (8,))
        lane = lax.broadcasted_iota(jnp.int32, s2.shape, 1)
        win_sum = jnp.concatenate([jnp.where(lane < HEAD_DIM, s2, s4),
                                   jnp.where(lane < HEAD_DIM, s8, s16)], axis=1)[HALO_C:, :]
        t_abs = s * tile + lax.broadcasted_iota(jnp.int32, (HALO_C, D_G), 0)
        window = jnp.left_shift(2, lax.broadcasted_iota(jnp.int32, (HALO_C, D_G), 1) // HEAD_DIM)
        count = jnp.minimum(t_abs + 1, window).astype(F32)
        inv_window = 1.0 / window[0:1, :].astype(F32)
        mean = jnp.concatenate([win_sum[:HALO_C, :] / count, win_sum[HALO_C:, :] * inv_window], axis=0)
        pooled = (mean - xc).astype(BF16)
        y_c = _dot(pooled, pool_w_ref[...]) * row(pool_scale_ref)
        xc_scr[0:HALO_C, :] = xc_scr[tile:tile + HALO_C, :]

        h_e = (y_e * gate(4 * D_G)).astype(BF16)
        refill(OFF_C, OFF_GATE + 4 * D_G)
        h_b = (y_b * gate(D_G)).astype(BF16)
        h_c = (y_c * gate(2 * D_G)).astype(BF16)
        refill(OFF_GATE + D_G, OFF_GATE + 2 * D_G)
        assert not refillable
        hcat_scr[...] = jnp.concatenate([h_a, h_b, h_c, h_d, h_e], axis=1)

        rows = tile // OUT_NORM_CHUNKS
        for c in range(OUT_NORM_CHUNKS):
            alpha8 = _sequenced_after(jnp.full((8, 128), alpha, F32), refill_done[OUT_NORM_FIRST_REFILL + c])
            alpha_t = jnp.concatenate([jnp.concatenate([alpha8] * (D_MODEL // 128), axis=1)] * (rows // 8), axis=0)
            out_norm(alpha_t, slice(c * rows, (c + 1) * rows))

    @pl.when(g == last)
    def _():
        out_matmul()
        out_norm(alpha)


def _layer_spec(shape, layer):
    resident = dict(pipeline_mode=pl.Buffered(1))
    if len(shape) == 2:
        return pl.BlockSpec(tuple(shape), lambda g: (0, 0), **resident)
    return pl.BlockSpec((None,) + tuple(shape[1:]), lambda g: (layer,) + (0,) * (len(shape) - 1), **resident)


def _layer_call(x, mem, params, layer, alpha):
    batch, seq, _ = x.shape
    tile = SEQ_TILE
    assert seq % tile == 0 and tile % CHUNK == 0 and tile % CONV_ROWS == 0
    n_seq = seq // tile
    n_tiles = batch * n_seq
    mem_len = mem.shape[1]

    def tile_at(g):
        g = jnp.clip(g, 0, n_tiles - 1)
        return g // n_seq, g % n_seq

    in_specs = [
        pl.BlockSpec((1, tile, D_MODEL), lambda g: (*tile_at(g - 1), 0)),
        pl.BlockSpec((1, tile, D_MODEL), lambda g: (*tile_at(g + 1), 0)),
        pl.BlockSpec((1, mem_len, D_MODEL), lambda g: (tile_at(g)[0], 0, 0)),
    ] + [_layer_spec(p.shape, layer) for p in params]
    return pl.pallas_call(
        functools.partial(_layer_kernel, tile=tile, n_seq=n_seq, layer=layer, alpha=alpha),
        grid=(n_tiles + 1,),
        in_specs=in_specs,
        out_specs=pl.BlockSpec((1, tile, D_MODEL), lambda g: (*tile_at(g - 1), 0)),
        out_shape=jax.ShapeDtypeStruct(x.shape, x.dtype),
        scratch_shapes=[
            pltpu.VMEM((tile, D_MODEL), BF16),
            pltpu.VMEM((tile, D_MIX), BF16),
            pltpu.VMEM((D_G, N_SUB * mem_len), BF16),
            pltpu.VMEM((N_SUB * mem_len, D_G), BF16),
            pltpu.VMEM((HALO_A + tile, D_G), F32),
            pltpu.VMEM((HALO_C + tile, D_G), F32),
            pltpu.VMEM((HALO_D + tile, D_G), F32),
        ] + [pltpu.VMEM((tile, width), F32) for _, width in PROJ_GROUPS],
        compiler_params=pltpu.CompilerParams(
            dimension_semantics=("arbitrary",),
            vmem_limit_bytes=VMEM_LIMIT_BYTES),
        name="mixer_layer",
    )(x, x, mem, *params)


def kernel(x, mem, w_in, conv_a_w, sg_ln_g, sg_ln_b, sg_w, sg_b, pool_w, pool_scale, cc_dw_w, cc_dw_b, cc_ln_g, cc_ln_b, cc_pw_w, w_kv, w_out, ln_g, ln_b):
    depth = w_in.shape[0]
    alpha = (2.0 * depth) ** 0.25
    wkt = w_kv[:, :, :D_G].transpose(0, 2, 1).astype(BF16)
    wv = w_kv[:, :, D_G:].astype(BF16)
    sg_w_cat = sg_w.transpose(0, 2, 1, 3).reshape(depth, CHUNK, N_SUB * CHUNK)
    sg_bias = jnp.repeat(sg_b.transpose(0, 2, 1), HEAD_DIM, axis=2)
    pool_bd = jnp.einsum("gh,lgcd->lgchd", jnp.eye(N_SUB, dtype=pool_w.dtype), pool_w)
    pool_bd = pool_bd.reshape(depth, D_G, D_G).astype(BF16)
    params = (
        wkt, wv, w_in.astype(BF16), conv_a_w, sg_ln_g, sg_ln_b, sg_w_cat, sg_bias,
        pool_bd, pool_scale, cc_dw_w, cc_dw_b, cc_ln_g, cc_ln_b,
        cc_pw_w.astype(BF16), w_out.astype(BF16), ln_g, ln_b,
    )
    for layer in range(depth):
        x = _layer_call(x, mem, params, layer, alpha)
    return x
```

```python
import functools

import jax
import jax.numpy as jnp
from jax import lax
from jax.experimental import pallas as pl
from jax.experimental.pallas import tpu as pltpu

D_MODEL = 1024
D_G = D_MODEL // 4
N_SUB = 4
HEAD_DIM = D_G // N_SUB
D_MIX = 5 * D_G
CHUNK = 128
CONV_A = 3
CONV_D = 31
POOL_WINDOWS = (2, 4, 8, 16)
LN_EPS = 1e-5

SUBLANES = 8
LANES = 128

OFF_A = 0
OFF_B = 3 * D_G
OFF_C = 5 * D_G
OFF_D = 6 * D_G
OFF_E = 8 * D_G
OFF_GATE = 9 * D_G

HALO_A = 8
HALO_C = 16
HALO_D = 32
CONV_ROWS = 32
CONV_CHAINS = 2
OUT_NORM_CHUNKS = 4
OUT_NORM_FIRST_REFILL = 2

PROJ_GROUPS = ((OFF_A, 3 * D_G), (OFF_B, 2 * D_G), (OFF_C, D_G), (OFF_D, 2 * D_G), (OFF_E, D_G)) + tuple(
    (OFF_GATE + g * D_G, D_G) for g in range(5))

SEQ_TILE = 512
VMEM_LIMIT_BYTES = 56 * 1024 * 1024

F32 = jnp.float32
BF16 = jnp.bfloat16


def _dot(a, b):
    return jnp.dot(a, b, preferred_element_type=F32)


def _layer_norm(x, g, b):
    mu = jnp.mean(x, axis=-1, keepdims=True)
    xc = x - mu
    var = jnp.mean(xc * xc, axis=-1, keepdims=True)
    return xc * lax.rsqrt(var + LN_EPS) * g + b


def _sequenced_after(x, done):
    words = pltpu.bitcast(done, jnp.uint32)
    word = words[0:SUBLANES, 0:LANES]
    for i in range(0, words.shape[0], SUBLANES):
        for j in range(0, words.shape[1], LANES):
            if i or j:
                word = word + words[i:i + SUBLANES, j:j + LANES]
    word = jnp.max(word.astype(jnp.int32), axis=-1, keepdims=True).astype(jnp.uint32)
    zero = lax.shift_right_logical(lax.shift_right_logical(word, jnp.uint32(16)), jnp.uint32(16))
    return pltpu.bitcast(pltpu.bitcast(x, jnp.uint32) | zero, F32)


def _project_memory(mem_ref, wkt_ref, wv_ref, kbd_scr, vbd_scr):
    m = mem_ref[0].astype(BF16)
    mem_len = m.shape[0]
    kt = lax.dot_general(wkt_ref[...], m, (((1,), (1,)), ((), ())),
                         preferred_element_type=F32)
    kt = kt * (1.0 / (HEAD_DIM ** 0.5))
    v = _dot(m, wv_ref[...])
    row_head = lax.broadcasted_iota(jnp.int32, kt.shape, 0) // HEAD_DIM
    col_head = lax.broadcasted_iota(jnp.int32, v.shape, 1) // HEAD_DIM
    for h in range(N_SUB):
        kbd_scr[:, h * mem_len:(h + 1) * mem_len] = jnp.where(row_head == h, kt, 0.0).astype(BF16)
        vbd_scr[h * mem_len:(h + 1) * mem_len, :] = jnp.where(col_head == h, v, 0.0).astype(BF16)


def _layer_kernel(xp_ref, xn_ref, mem_ref, wkt_ref, wv_ref, w_in_ref, conv_a_ref, sg_g_ref, sg_b_ref, sg_w_ref,
                  sg_bias_ref, pool_w_ref, pool_scale_ref, dw_w_ref, dw_b_ref, cc_g_ref, cc_b_ref,
                  pw_ref, w_out_ref, ln_g_ref, ln_b_ref, o_ref, xb_scr, hcat_scr, kbd_scr, vbd_scr, cx_scr, xc_scr, h_scr,
                  *proj_scrs, tile, n_seq, layer, alpha):
    g = pl.program_id(0)
    last = pl.num_programs(0) - 1
    s = g % n_seq
    proj_of = dict(zip(PROJ_GROUPS, proj_scrs))

    def row(ref):
        return ref[layer:layer + 1, :]

    @pl.when(g == 0)
    def _():
        xb0 = xp_ref[0].astype(BF16)
        for (lo, width), scr in proj_of.items():
            scr[...] = _dot(xb0, w_in_ref[:, lo:lo + width])
        hcat_scr[...] = jnp.zeros(hcat_scr.shape, BF16)

    @pl.when(s == 0)
    def _():
        _project_memory(mem_ref, wkt_ref, wv_ref, kbd_scr, vbd_scr)
        cx_scr[0:HALO_A, :] = jnp.zeros((HALO_A, D_G), F32)
        xc_scr[0:HALO_C, :] = jnp.zeros((HALO_C, D_G), F32)
        h_scr[0:HALO_D, :] = jnp.zeros((HALO_D, D_G), F32)

    def out_matmul():
        o_ref[0] = _dot(hcat_scr[...], w_out_ref[...])

    def out_norm(alpha_x, rows=slice(None)):
        o_ref[0, rows, :] = _layer_norm(alpha_x * xp_ref[0, rows, :] + o_ref[0, rows, :],
                                        row(ln_g_ref), row(ln_b_ref))

    @pl.when(g < last)
    def _():
        def proj_tile(lo):
            group = [grp for grp in PROJ_GROUPS if grp[0] <= lo < grp[0] + grp[1]][0]
            return proj_of[group], lo - group[0]

        def take(lo, width):
            scr, off = proj_tile(lo)
            return scr[:, off:off + width]

        refillable = []
        refill_done = []

        def refill(*tiles):
            refillable.extend(tiles)
            while len(refillable) >= 2:
                pair = (refillable.pop(0), refillable.pop(0))
                w = jnp.concatenate([w_in_ref[:, t:t + D_G] for t in pair], axis=1)
                res = _dot(xb_scr[...], w)
                for i, t in enumerate(pair):
                    dst, dst_off = proj_tile(t)
                    dst[:, dst_off:dst_off + D_G] = res[:, i * D_G:(i + 1) * D_G]
                refill_done.append(res[tile - SUBLANES:, 2 * D_G - LANES:])

        def gate(lo):
            return jax.nn.silu(take(OFF_GATE + lo, D_G))

        xb_scr[...] = xn_ref[0].astype(BF16)

        out_matmul()

        pd = take(OFF_D, 2 * D_G)
        h_scr[HALO_D:HALO_D + tile, :] = pd[:, :D_G] * jax.nn.sigmoid(pd[:, D_G:])

        pa = take(OFF_A, 3 * D_G)
        xa, ba, ca = pa[:, :D_G], pa[:, D_G:2 * D_G], pa[:, 2 * D_G:]
        cx = ca * xa
        cx_scr[HALO_A:HALO_A + tile, :] = cx
        conv_a = conv_a_ref[2:3, :] * cx
        for k in range(CONV_A - 1):
            off = HALO_A - (CONV_A - 1) + k
            conv_a = conv_a + conv_a_ref[k:k + 1, :] * cx_scr[off:off + tile, :]
        h_a = (ba * conv_a * gate(0)).astype(BF16)
        cx_scr[0:HALO_A, :] = cx_scr[tile:tile + HALO_A, :]
        q = take(OFF_E, D_G).astype(BF16)

        lead = HALO_D - (CONV_D - 1)
        conv_d = []
        for c in range(tile // CONV_ROWS):
            base = c * CONV_ROWS
            bias = jnp.broadcast_to(row(dw_b_ref), (SUBLANES, D_G))
            if len(conv_d) >= CONV_CHAINS:
                bias = _sequenced_after(bias, conv_d[-CONV_CHAINS])
            acc = jnp.concatenate([bias] * (CONV_ROWS // SUBLANES), axis=0)
            for r in range(SUBLANES):
                rows = CONV_ROWS + (SUBLANES if r else 0)
                part = None
                for j in range(r, lead + CONV_D, SUBLANES):
                    if j < lead:
                        continue
                    term = dw_w_ref[j - lead:j - lead + 1, :] * h_scr[base + j - r:base + j - r + rows, :]
                    part = term if part is None else part + term
                acc = acc + part[r:r + CONV_ROWS, :]
            conv_d.append(acc)
        h_scr[0:HALO_D, :] = h_scr[tile:tile + HALO_D, :]
        refill(OFF_D, OFF_D + D_G, OFF_A, OFF_A + D_G)
        hn = jax.nn.silu(_layer_norm(jnp.concatenate(conv_d, axis=0), row(cc_g_ref), row(cc_b_ref)))
        y_d = _dot(hn.astype(BF16), pw_ref[...])
        scores = _dot(q, kbd_scr[...])
        refill(OFF_A + 2 * D_G, OFF_GATE)
        h_d = (y_d * gate(3 * D_G)).astype(BF16)

        mem_len = scores.shape[1] // N_SUB
        probs = []
        for h in range(N_SUB):
            sh = scores[:, h * mem_len:(h + 1) * mem_len]
            e = jnp.exp(sh - jnp.max(sh, axis=-1, keepdims=True))
            inv = 1.0 / jnp.sum(e, axis=-1, keepdims=True)
            probs.append((e * inv).astype(BF16))
        refill(OFF_GATE + 3 * D_G, OFF_E)
        y_e = _dot(jnp.concatenate(probs, axis=1), vbd_scr[...])

        pb = take(OFF_B, 2 * D_G)
        u = jax.nn.gelu(pb[:, :D_G])
        v = _layer_norm(jax.nn.gelu(pb[:, D_G:]), row(sg_g_ref), row(sg_b_ref))
        vb = v.astype(BF16)
        refill(OFF_B, OFF_B + D_G)
        wt = lax.broadcasted_iota(jnp.int32, (CHUNK, N_SUB * CHUNK), 0)
        ws = lax.broadcasted_iota(jnp.int32, (CHUNK, N_SUB * CHUNK), 1) % CHUNK
        w_mix = jnp.where(ws <= wt, sg_w_ref[...], 0.0).astype(BF16)
        lane_head = lax.broadcasted_iota(jnp.int32, (CHUNK, D_G), 1) // HEAD_DIM
        mixed = []
        for c in range(tile // CHUNK):
            vc = vb[c * CHUNK:(c + 1) * CHUNK, :]
            rhs = jnp.concatenate(
                [jnp.where(lane_head == h, vc, jnp.zeros_like(vc)) for h in range(N_SUB)], axis=0)
            mixed.append(_dot(w_mix, rhs) + sg_bias_ref[...])
        y_b = u * jnp.concatenate(mixed, axis=0)

        xc = take(OFF_C, D_G)
        xc_scr[HALO_C:HALO_C + tile, :] = xc

        xe = xc_scr[...]

        def doubled(val, steps):
            for shift in steps:
                val = val + pltpu.roll(val, shift, axis=0)
            return val

        s2 = doubled(xe[:, :2 * HEAD_DIM], (1,))
        s4 = doubled(s2, (2,))
        s8 = doubled(xe[:, 2 * HEAD_DIM:], (1, 2, 4))
        s16 = doubled(s8, (8,))
        lane = lax.broadcasted_iota(jnp.int32, s2.shape, 1)
        win_sum = jnp.concatenate([jnp.where(lane < HEAD_DIM, s2, s4),
                                   jnp.where(lane < HEAD_DIM, s8, s16)], axis=1)[HALO_C:, :]
        t_abs = s * tile + lax.broadcasted_iota(jnp.int32, (HALO_C, D_G), 0)
        window = jnp.left_shift(2, lax.broadcasted_iota(jnp.int32, (HALO_C, D_G), 1) // HEAD_DIM)
        count = jnp.minimum(t_abs + 1, window).astype(F32)
        inv_window = 1.0 / window[0:1, :].astype(F32)
        mean = jnp.concatenate([win_sum[:HALO_C, :] / count, win_sum[HALO_C:, :] * inv_window], axis=0)
        pooled = (mean - xc).astype(BF16)
        y_c = _dot(pooled, pool_w_ref[...]) * row(pool_scale_ref)
        xc_scr[0:HALO_C, :] = xc_scr[tile:tile + HALO_C, :]

        h_e = (y_e * gate(4 * D_G)).astype(BF16)
        refill(OFF_C, OFF_GATE + 4 * D_G)
        h_b = (y_b * gate(D_G)).astype(BF16)
        h_c = (y_c * gate(2 * D_G)).astype(BF16)
        refill(OFF_GATE + D_G, OFF_GATE + 2 * D_G)
        assert not refillable
        hcat_scr[...] = jnp.concatenate([h_a, h_b, h_c, h_d, h_e], axis=1)

        rows = tile // OUT_NORM_CHUNKS
        for c in range(OUT_NORM_CHUNKS):
            alpha1 = _sequenced_after(jnp.full((SUBLANES, LANES), alpha, F32), refill_done[OUT_NORM_FIRST_REFILL + c])
            alpha_t = jnp.concatenate([jnp.concatenate([alpha1] * (D_MODEL // LANES), axis=1)] * (rows // SUBLANES), axis=0)
            out_norm(alpha_t, slice(c * rows, (c + 1) * rows))

    @pl.when(g == last)
    def _():
        out_matmul()
        out_norm(alpha)


def _layer_spec(shape, layer):
    resident = dict(pipeline_mode=pl.Buffered(1))
    if len(shape) == 2:
        return pl.BlockSpec(tuple(shape), lambda g: (0, 0), **resident)
    return pl.BlockSpec((None,) + tuple(shape[1:]), lambda g: (layer,) + (0,) * (len(shape) - 1), **resident)


def _layer_call(x, mem, params, layer, alpha):
    batch, seq, _ = x.shape
    tile = SEQ_TILE
    assert seq % tile == 0 and tile % CHUNK == 0 and tile % CONV_ROWS == 0
    assert POOL_WINDOWS == tuple(2 << grp for grp in range(N_SUB)) and 2 * HEAD_DIM == LANES
    n_seq = seq // tile
    n_tiles = batch * n_seq
    mem_len = mem.shape[1]

    def tile_at(g):
        g = jnp.clip(g, 0, n_tiles - 1)
        return g // n_seq, g % n_seq

    in_specs = [
        pl.BlockSpec((1, tile, D_MODEL), lambda g: (*tile_at(g - 1), 0)),
        pl.BlockSpec((1, tile, D_MODEL), lambda g: (*tile_at(g + 1), 0)),
        pl.BlockSpec((1, mem_len, D_MODEL), lambda g: (tile_at(g)[0], 0, 0)),
    ] + [_layer_spec(p.shape, layer) for p in params]
    return pl.pallas_call(
        functools.partial(_layer_kernel, tile=tile, n_seq=n_seq, layer=layer, alpha=alpha),
        grid=(n_tiles + 1,),
        in_specs=in_specs,
        out_specs=pl.BlockSpec((1, tile, D_MODEL), lambda g: (*tile_at(g - 1), 0)),
        out_shape=jax.ShapeDtypeStruct(x.shape, x.dtype),
        scratch_shapes=[
            pltpu.VMEM((tile, D_MODEL), BF16),
            pltpu.VMEM((tile, D_MIX), BF16),
            pltpu.VMEM((D_G, N_SUB * mem_len), BF16),
            pltpu.VMEM((N_SUB * mem_len, D_G), BF16),
            pltpu.VMEM((HALO_A + tile, D_G), F32),
            pltpu.VMEM((HALO_C + tile, D_G), F32),
            pltpu.VMEM((HALO_D + tile, D_G), F32),
        ] + [pltpu.VMEM((tile, width), F32) for _, width in PROJ_GROUPS],
        compiler_params=pltpu.CompilerParams(
            dimension_semantics=("arbitrary",),
            vmem_limit_bytes=VMEM_LIMIT_BYTES),
        name="mixer_layer",
    )(x, x, mem, *params)


def kernel(x, mem, w_in, conv_a_w, sg_ln_g, sg_ln_b, sg_w, sg_b, pool_w, pool_scale, cc_dw_w, cc_dw_b, cc_ln_g, cc_ln_b, cc_pw_w, w_kv, w_out, ln_g, ln_b):
    depth = w_in.shape[0]
    alpha = (2.0 * depth) ** 0.25
    wkt = w_kv[:, :, :D_G].transpose(0, 2, 1).astype(BF16)
    wv = w_kv[:, :, D_G:].astype(BF16)
    sg_w_cat = sg_w.transpose(0, 2, 1, 3).reshape(depth, CHUNK, N_SUB * CHUNK)
    sg_bias = jnp.repeat(sg_b.transpose(0, 2, 1), HEAD_DIM, axis=2)
    pool_bd = jnp.einsum("gh,lgcd->lgchd", jnp.eye(N_SUB, dtype=pool_w.dtype), pool_w)
    pool_bd = pool_bd.reshape(depth, D_G, D_G).astype(BF16)
    params = (
        wkt, wv, w_in.astype(BF16), conv_a_w, sg_ln_g, sg_ln_b, sg_w_cat, sg_bias,
        pool_bd, pool_scale, cc_dw_w, cc_dw_b, cc_ln_g, cc_ln_b,
        cc_pw_w.astype(BF16), w_out.astype(BF16), ln_g, ln_b,
    )
    for layer in range(depth):
        x = _layer_call(x, mem, params, layer, alpha)
    return x
```

```python
import functools

import jax
import jax.numpy as jnp
from jax import lax
from jax.experimental import pallas as pl
from jax.experimental.pallas import tpu as pltpu

D_MODEL = 1024
D_G = D_MODEL // 4
N_SUB = 4
HEAD_DIM = D_G // N_SUB
D_MIX = 5 * D_G
CHUNK = 128
CONV_A = 3
CONV_D = 31
POOL_WINDOWS = (2, 4, 8, 16)
LN_EPS = 1e-5

SUBLANES = 8
LANES = 128

OFF_A = 0
OFF_B = 3 * D_G
OFF_C = 5 * D_G
OFF_D = 6 * D_G
OFF_E = 8 * D_G
OFF_GATE = 9 * D_G

HALO_A = 8
HALO_C = 16
HALO_D = 32
CONV_ROWS = 32
CONV_CHAINS = 2
OUT_NORM_CHUNKS = 4
OUT_NORM_FIRST_REFILL = 2

PROJ_GROUPS = ((OFF_A, 3 * D_G), (OFF_B, 2 * D_G), (OFF_C, D_G), (OFF_D, 2 * D_G), (OFF_E, D_G)) + tuple(
    (OFF_GATE + g * D_G, D_G) for g in range(5))

SEQ_TILE = 512
VMEM_LIMIT_BYTES = 56 * 1024 * 1024

F32 = jnp.float32
BF16 = jnp.bfloat16


def _dot(a, b):
    return jnp.dot(a, b, preferred_element_type=F32)


def _layer_norm(x, g, b):
    mu = jnp.mean(x, axis=-1, keepdims=True)
    xc = x - mu
    var = jnp.mean(xc * xc, axis=-1, keepdims=True)
    return xc * lax.rsqrt(var + LN_EPS) * g + b


def _sequenced_after(x, done):
    words = pltpu.bitcast(done, jnp.uint32)
    word = words[0:SUBLANES, 0:LANES]
    for i in range(0, words.shape[0], SUBLANES):
        for j in range(0, words.shape[1], LANES):
            if i or j:
                word = word + words[i:i + SUBLANES, j:j + LANES]
    word = jnp.max(word.astype(jnp.int32), axis=-1, keepdims=True).astype(jnp.uint32)
    zero = lax.shift_right_logical(lax.shift_right_logical(word, jnp.uint32(16)), jnp.uint32(16))
    return pltpu.bitcast(pltpu.bitcast(x, jnp.uint32) | zero, F32)


def _project_memory(mem_ref, wkt_ref, wv_ref, kbd_scr, vbd_scr):
    m = mem_ref[0].astype(BF16)
    mem_len = m.shape[0]
    kt = lax.dot_general(wkt_ref[...], m, (((1,), (1,)), ((), ())),
                         preferred_element_type=F32)
    kt = kt * (1.0 / (HEAD_DIM ** 0.5))
    v = _dot(m, wv_ref[...])
    row_head = lax.broadcasted_iota(jnp.int32, kt.shape, 0) // HEAD_DIM
    col_head = lax.broadcasted_iota(jnp.int32, v.shape, 1) // HEAD_DIM
    for h in range(N_SUB):
        kbd_scr[:, h * mem_len:(h + 1) * mem_len] = jnp.where(row_head == h, kt, 0.0).astype(BF16)
        vbd_scr[h * mem_len:(h + 1) * mem_len, :] = jnp.where(col_head == h, v, 0.0).astype(BF16)


def _layer_kernel(xp_ref, xn_ref, mem_ref, wkt_ref, wv_ref, w_in_ref, conv_a_ref, sg_g_ref, sg_b_ref, sg_w_ref,
                  sg_bias_ref, pool_w_ref, pool_scale_ref, dw_w_ref, dw_b_ref, cc_g_ref, cc_b_ref,
                  pw_ref, w_out_ref, ln_g_ref, ln_b_ref, o_ref, xb_scr, hcat_scr, kbd_scr, vbd_scr, cx_scr, xc_scr, h_scr,
                  *proj_scrs, tile, n_seq, layer, alpha):
    g = pl.program_id(0)
    last = pl.num_programs(0) - 1
    s = g % n_seq
    proj_of = dict(zip(PROJ_GROUPS, proj_scrs))

    def row(ref):
        return ref[layer:layer + 1, :]

    @pl.when(g == 0)
    def _():
        xb0 = xp_ref[0].astype(BF16)
        for (lo, width), scr in proj_of.items():
            scr[...] = _dot(xb0, w_in_ref[:, lo:lo + width])
        hcat_scr[...] = jnp.zeros(hcat_scr.shape, BF16)

    @pl.when(s == 0)
    def _():
        _project_memory(mem_ref, wkt_ref, wv_ref, kbd_scr, vbd_scr)
        cx_scr[0:HALO_A, :] = jnp.zeros((HALO_A, D_G), F32)
        xc_scr[0:HALO_C, :] = jnp.zeros((HALO_C, D_G), F32)
        h_scr[0:HALO_D, :] = jnp.zeros((HALO_D, D_G), F32)

    def out_matmul():
        o_ref[0] = lax.dot_general(hcat_scr[...], w_out_ref[...], (((1,), (1,)), ((), ())),
                                   preferred_element_type=F32)

    def out_norm(alpha_x, rows=slice(None)):
        o_ref[0, rows, :] = _layer_norm(alpha_x * xp_ref[0, rows, :] + o_ref[0, rows, :],
                                        row(ln_g_ref), row(ln_b_ref))

    @pl.when(g < last)
    def _():
        def proj_tile(lo):
            group = [grp for grp in PROJ_GROUPS if grp[0] <= lo < grp[0] + grp[1]][0]
            return proj_of[group], lo - group[0]

        def take(lo, width):
            scr, off = proj_tile(lo)
            return scr[:, off:off + width]

        refillable = []
        refill_done = []

        def refill(*tiles):
            refillable.extend(tiles)
            while len(refillable) >= 2:
                pair = (refillable.pop(0), refillable.pop(0))
                w = jnp.concatenate([w_in_ref[:, t:t + D_G] for t in pair], axis=1)
                res = _dot(xb_scr[...], w)
                for i, t in enumerate(pair):
                    dst, dst_off = proj_tile(t)
                    dst[:, dst_off:dst_off + D_G] = res[:, i * D_G:(i + 1) * D_G]
                refill_done.append(res[tile - SUBLANES:, 2 * D_G - LANES:])

        def gate(lo):
            return jax.nn.silu(take(OFF_GATE + lo, D_G))

        xb_scr[...] = xn_ref[0].astype(BF16)

        out_matmul()

        pd = take(OFF_D, 2 * D_G)
        h_scr[HALO_D:HALO_D + tile, :] = pd[:, :D_G] * jax.nn.sigmoid(pd[:, D_G:])

        pa = take(OFF_A, 3 * D_G)
        xa, ba, ca = pa[:, :D_G], pa[:, D_G:2 * D_G], pa[:, 2 * D_G:]
        cx = ca * xa
        cx_scr[HALO_A:HALO_A + tile, :] = cx
        conv_a = conv_a_ref[2:3, :] * cx
        for k in range(CONV_A - 1):
            off = HALO_A - (CONV_A - 1) + k
            conv_a = conv_a + conv_a_ref[k:k + 1, :] * cx_scr[off:off + tile, :]
        h_a = (ba * conv_a * gate(0)).astype(BF16)
        cx_scr[0:HALO_A, :] = cx_scr[tile:tile + HALO_A, :]
        q = take(OFF_E, D_G).astype(BF16)

        lead = HALO_D - (CONV_D - 1)
        conv_d = []
        for c in range(tile // CONV_ROWS):
            base = c * CONV_ROWS
            bias = jnp.broadcast_to(row(dw_b_ref), (SUBLANES, D_G))
            if len(conv_d) >= CONV_CHAINS:
                bias = _sequenced_after(bias, conv_d[-CONV_CHAINS])
            acc = jnp.concatenate([bias] * (CONV_ROWS // SUBLANES), axis=0)
            for r in range(SUBLANES):
                rows = CONV_ROWS + (SUBLANES if r else 0)
                part = None
                for j in range(r, lead + CONV_D, SUBLANES):
                    if j < lead:
                        continue
                    term = dw_w_ref[j - lead:j - lead + 1, :] * h_scr[base + j - r:base + j - r + rows, :]
                    part = term if part is None else part + term
                acc = acc + part[r:r + CONV_ROWS, :]
            conv_d.append(acc)
        h_scr[0:HALO_D, :] = h_scr[tile:tile + HALO_D, :]
        refill(OFF_D, OFF_D + D_G, OFF_A, OFF_A + D_G)
        hn = jax.nn.silu(_layer_norm(jnp.concatenate(conv_d, axis=0), row(cc_g_ref), row(cc_b_ref)))
        y_d = _dot(hn.astype(BF16), pw_ref[...])
        scores = _dot(q, kbd_scr[...])
        refill(OFF_A + 2 * D_G, OFF_GATE)
        h_d = (y_d * gate(3 * D_G)).astype(BF16)

        mem_len = scores.shape[1] // N_SUB
        probs = []
        for h in range(N_SUB):
            sh = scores[:, h * mem_len:(h + 1) * mem_len]
            e = jnp.exp(sh - jnp.max(sh, axis=-1, keepdims=True))
            inv = 1.0 / jnp.sum(e, axis=-1, keepdims=True)
            probs.append((e * inv).astype(BF16))
        refill(OFF_GATE + 3 * D_G, OFF_E)
        y_e = _dot(jnp.concatenate(probs, axis=1), vbd_scr[...])

        pb = take(OFF_B, 2 * D_G)
        u = jax.nn.gelu(pb[:, :D_G])
        v = _layer_norm(jax.nn.gelu(pb[:, D_G:]), row(sg_g_ref), row(sg_b_ref))
        vb = v.astype(BF16)
        refill(OFF_B, OFF_B + D_G)
        wt = lax.broadcasted_iota(jnp.int32, (CHUNK, N_SUB * CHUNK), 0)
        ws = lax.broadcasted_iota(jnp.int32, (CHUNK, N_SUB * CHUNK), 1) % CHUNK
        w_mix = jnp.where(ws <= wt, sg_w_ref[...], 0.0).astype(BF16)
        lane_head = lax.broadcasted_iota(jnp.int32, (CHUNK, D_G), 1) // HEAD_DIM
        mixed = []
        for c in range(tile // CHUNK):
            vc = vb[c * CHUNK:(c + 1) * CHUNK, :]
            rhs = jnp.concatenate(
                [jnp.where(lane_head == h, vc, jnp.zeros_like(vc)) for h in range(N_SUB)], axis=0)
            mixed.append(_dot(w_mix, rhs) + sg_bias_ref[...])
        y_b = u * jnp.concatenate(mixed, axis=0)

        xc = take(OFF_C, D_G)
        xc_scr[HALO_C:HALO_C + tile, :] = xc

        xe = xc_scr[...]

        def doubled(val, steps):
            for shift in steps:
                val = val + pltpu.roll(val, shift, axis=0)
            return val

        s2 = doubled(xe[:, :2 * HEAD_DIM], (1,))
        s4 = doubled(s2, (2,))
        s8 = doubled(xe[:, 2 * HEAD_DIM:], (1, 2, 4))
        s16 = doubled(s8, (8,))
        lane = lax.broadcasted_iota(jnp.int32, s2.shape, 1)
        win_sum = jnp.concatenate([jnp.where(lane < HEAD_DIM, s2, s4),
                                   jnp.where(lane < HEAD_DIM, s8, s16)], axis=1)[HALO_C:, :]
        t_abs = s * tile + lax.broadcasted_iota(jnp.int32, (HALO_C, D_G), 0)
        window = jnp.left_shift(2, lax.broadcasted_iota(jnp.int32, (HALO_C, D_G), 1) // HEAD_DIM)
        count = jnp.minimum(t_abs + 1, window).astype(F32)
        inv_window = 1.0 / window[0:1, :].astype(F32)
        mean = jnp.concatenate([win_sum[:HALO_C, :] / count, win_sum[HALO_C:, :] * inv_window], axis=0)
        pooled = (mean - xc).astype(BF16)
        y_c = _dot(pooled, pool_w_ref[...]) * row(pool_scale_ref)
        xc_scr[0:HALO_C, :] = xc_scr[tile:tile + HALO_C, :]

        h_e = (y_e * gate(4 * D_G)).astype(BF16)
        refill(OFF_C, OFF_GATE + 4 * D_G)
        h_b = (y_b * gate(D_G)).astype(BF16)
        h_c = (y_c * gate(2 * D_G)).astype(BF16)
        refill(OFF_GATE + D_G, OFF_GATE + 2 * D_G)
        assert not refillable
        hcat_scr[...] = jnp.concatenate([h_a, h_b, h_c, h_d, h_e], axis=1)

        rows = tile // OUT_NORM_CHUNKS
        for c in range(OUT_NORM_CHUNKS):
            alpha1 = _sequenced_after(jnp.full((SUBLANES, LANES), alpha, F32), refill_done[OUT_NORM_FIRST_REFILL + c])
            alpha_t = jnp.concatenate([jnp.concatenate([alpha1] * (D_MODEL // LANES), axis=1)] * (rows // SUBLANES), axis=0)
            out_norm(alpha_t, slice(c * rows, (c + 1) * rows))

    @pl.when(g == last)
    def _():
        out_matmul()
        out_norm(alpha)


def _layer_spec(shape, layer):
    resident = dict(pipeline_mode=pl.Buffered(1))
    if len(shape) == 2:
        return pl.BlockSpec(tuple(shape), lambda g: (0, 0), **resident)
    return pl.BlockSpec((None,) + tuple(shape[1:]), lambda g: (layer,) + (0,) * (len(shape) - 1), **resident)


def _layer_call(x, mem, params, layer, alpha):
    batch, seq, _ = x.shape
    tile = SEQ_TILE
    assert seq % tile == 0 and tile % CHUNK == 0 and tile % CONV_ROWS == 0
    assert POOL_WINDOWS == tuple(2 << grp for grp in range(N_SUB)) and 2 * HEAD_DIM == LANES
    n_seq = seq // tile
    n_tiles = batch * n_seq
    mem_len = mem.shape[1]

    def tile_at(g):
        g = jnp.clip(g, 0, n_tiles - 1)
        return g // n_seq, g % n_seq

    in_specs = [
        pl.BlockSpec((1, tile, D_MODEL), lambda g: (*tile_at(g - 1), 0)),
        pl.BlockSpec((1, tile, D_MODEL), lambda g: (*tile_at(g + 1), 0)),
        pl.BlockSpec((1, mem_len, D_MODEL), lambda g: (tile_at(g)[0], 0, 0)),
    ] + [_layer_spec(p.shape, layer) for p in params]
    return pl.pallas_call(
        functools.partial(_layer_kernel, tile=tile, n_seq=n_seq, layer=layer, alpha=alpha),
        grid=(n_tiles + 1,),
        in_specs=in_specs,
        out_specs=pl.BlockSpec((1, tile, D_MODEL), lambda g: (*tile_at(g - 1), 0)),
        out_shape=jax.ShapeDtypeStruct(x.shape, x.dtype),
        scratch_shapes=[
            pltpu.VMEM((tile, D_MODEL), BF16),
            pltpu.VMEM((tile, D_MIX), BF16),
            pltpu.VMEM((D_G, N_SUB * mem_len), BF16),
            pltpu.VMEM((N_SUB * mem_len, D_G), BF16),
            pltpu.VMEM((HALO_A + tile, D_G), F32),
            pltpu.VMEM((HALO_C + tile, D_G), F32),
            pltpu.VMEM((HALO_D + tile, D_G), F32),
        ] + [pltpu.VMEM((tile, width), F32) for _, width in PROJ_GROUPS],
        compiler_params=pltpu.CompilerParams(
            dimension_semantics=("arbitrary",),
            vmem_limit_bytes=VMEM_LIMIT_BYTES),
        name="mixer_layer",
    )(x, x, mem, *params)


def kernel(x, mem, w_in, conv_a_w, sg_ln_g, sg_ln_b, sg_w, sg_b, pool_w, pool_scale, cc_dw_w, cc_dw_b, cc_ln_g, cc_ln_b, cc_pw_w, w_kv, w_out, ln_g, ln_b):
    depth = w_in.shape[0]
    alpha = (2.0 * depth) ** 0.25
    wkt = w_kv[:, :, :D_G].transpose(0, 2, 1).astype(BF16)
    wv = w_kv[:, :, D_G:].astype(BF16)
    sg_w_cat = sg_w.transpose(0, 2, 1, 3).reshape(depth, CHUNK, N_SUB * CHUNK)
    sg_bias = jnp.repeat(sg_b.transpose(0, 2, 1), HEAD_DIM, axis=2)
    pool_bd = jnp.einsum("gh,lgcd->lgchd", jnp.eye(N_SUB, dtype=pool_w.dtype), pool_w)
    pool_bd = pool_bd.reshape(depth, D_G, D_G).astype(BF16)
    params = (
        wkt, wv, w_in.astype(BF16), conv_a_w, sg_ln_g, sg_ln_b, sg_w_cat, sg_bias,
        pool_bd, pool_scale, cc_dw_w, cc_dw_b, cc_ln_g, cc_ln_b,
        cc_pw_w.astype(BF16), w_out.transpose(0, 2, 1).astype(BF16), ln_g, ln_b,
    )
    for layer in range(depth):
        x = _layer_call(x, mem, params, layer, alpha)
    return x
```

```python
import functools

import jax
import jax.numpy as jnp
from jax import lax
from jax.experimental import pallas as pl
from jax.experimental.pallas import tpu as pltpu

D_MODEL = 1024
D_G = D_MODEL // 4
N_SUB = 4
HEAD_DIM = D_G // N_SUB
D_MIX = 5 * D_G
D_IN = 9 * D_G + D_MIX
CHUNK = 128
CONV_A = 3
CONV_D = 31
POOL_WINDOWS = (2, 4, 8, 16)
LN_EPS = 1e-5

SUBLANES = 8
LANES = 128

OFF_A = 0
OFF_B = 3 * D_G
OFF_C = 5 * D_G
OFF_D = 6 * D_G
OFF_E = 8 * D_G
OFF_GATE = 9 * D_G

HALO_A = 8
HALO_C = 16
HALO_D = 32
W_STAGE_COLS = 2 * D_G
CONV_ROWS = 32
CONV_CHAINS = 2
OUT_NORM_CHUNKS = 4
OUT_NORM_FIRST_REFILL = 2

PROJ_GROUPS = ((OFF_A, 3 * D_G), (OFF_B, 2 * D_G), (OFF_C, D_G), (OFF_D, 2 * D_G), (OFF_E, D_G)) + tuple(
    (OFF_GATE + g * D_G, D_G) for g in range(5))

SEQ_TILE = 512
VMEM_LIMIT_BYTES = 56 * 1024 * 1024

F32 = jnp.float32
BF16 = jnp.bfloat16


def _dot(a, b):
    return jnp.dot(a, b, preferred_element_type=F32)


def _layer_norm(x, g, b):
    mu = jnp.mean(x, axis=-1, keepdims=True)
    xc = x - mu
    var = jnp.mean(xc * xc, axis=-1, keepdims=True)
    return xc * lax.rsqrt(var + LN_EPS) * g + b


def _sequenced_after(x, done):
    words = pltpu.bitcast(done, jnp.uint32)
    word = words[0:SUBLANES, 0:LANES]
    for i in range(0, words.shape[0], SUBLANES):
        for j in range(0, words.shape[1], LANES):
            if i or j:
                word = word + words[i:i + SUBLANES, j:j + LANES]
    word = jnp.max(word.astype(jnp.int32), axis=-1, keepdims=True).astype(jnp.uint32)
    zero = lax.shift_right_logical(lax.shift_right_logical(word, jnp.uint32(16)), jnp.uint32(16))
    return pltpu.bitcast(pltpu.bitcast(x, jnp.uint32) | zero, F32)


def _project_memory(mem_ref, wkt_ref, wv_ref, kbd_scr, vbd_scr):
    m = mem_ref[0].astype(BF16)
    mem_len = m.shape[0]
    kt = lax.dot_general(wkt_ref[...], m, (((1,), (1,)), ((), ())),
                         preferred_element_type=F32)
    kt = kt * (1.0 / (HEAD_DIM ** 0.5))
    v = _dot(m, wv_ref[...])
    row_head = lax.broadcasted_iota(jnp.int32, kt.shape, 0) // HEAD_DIM
    col_head = lax.broadcasted_iota(jnp.int32, v.shape, 1) // HEAD_DIM
    for h in range(N_SUB):
        kbd_scr[:, h * mem_len:(h + 1) * mem_len] = jnp.where(row_head == h, kt, 0.0).astype(BF16)
        vbd_scr[h * mem_len:(h + 1) * mem_len, :] = jnp.where(col_head == h, v, 0.0).astype(BF16)


def _layer_kernel(xp_ref, xn_ref, mem_ref, w_in_hbm, wkt_ref, wv_ref, conv_a_ref, sg_g_ref, sg_b_ref, sg_w_ref,
                  sg_bias_ref, pool_w_ref, pool_scale_ref, dw_w_ref, dw_b_ref, cc_g_ref, cc_b_ref,
                  pw_ref, w_out_ref, ln_g_ref, ln_b_ref, o_ref, w_in_ref, w_stage, w_sem, xb_scr, hcat_scr, kbd_scr, vbd_scr,
                  cx_scr, xc_scr, h_scr, *proj_scrs, tile, n_seq, layer, alpha):
    g = pl.program_id(0)
    last = pl.num_programs(0) - 1
    s = g % n_seq
    proj_of = dict(zip(PROJ_GROUPS, proj_scrs))

    def row(ref):
        return ref[layer:layer + 1, :]

    def proj_tile(lo):
        group = [grp for grp in PROJ_GROUPS if grp[0] <= lo < grp[0] + grp[1]][0]
        return proj_of[group], lo - group[0]

    def w_in_copy(c):
        return pltpu.make_async_copy(
            w_in_hbm.at[layer, :, pl.ds(c * W_STAGE_COLS, W_STAGE_COLS)], w_stage.at[c % 2], w_sem.at[c % 2])

    @pl.when(g == 0)
    def _():
        n_chunks = D_IN // W_STAGE_COLS
        w_in_copy(0).start()
        xb0 = xp_ref[0].astype(BF16)
        for c in range(n_chunks + 1):
            if c + 1 < n_chunks:
                w_in_copy(c + 1).start()
            if c < n_chunks:
                w_in_copy(c).wait()
                w_in_ref[:, c * W_STAGE_COLS:(c + 1) * W_STAGE_COLS] = w_stage[c % 2].astype(BF16)
            if c:
                lo = (c - 1) * W_STAGE_COLS
                res = _dot(xb0, w_in_ref[:, lo:lo + W_STAGE_COLS])
                for t in range(0, W_STAGE_COLS, D_G):
                    dst, off = proj_tile(lo + t)
                    dst[:, off:off + D_G] = res[:, t:t + D_G]
        hcat_scr[...] = jnp.zeros(hcat_scr.shape, BF16)

    @pl.when(s == 0)
    def _():
        _project_memory(mem_ref, wkt_ref, wv_ref, kbd_scr, vbd_scr)
        cx_scr[0:HALO_A, :] = jnp.zeros((HALO_A, D_G), F32)
        xc_scr[0:HALO_C, :] = jnp.zeros((HALO_C, D_G), F32)
        h_scr[0:HALO_D, :] = jnp.zeros((HALO_D, D_G), F32)

    def out_matmul():
        o_ref[0] = lax.dot_general(hcat_scr[...], w_out_ref[...], (((1,), (1,)), ((), ())),
                                   preferred_element_type=F32)

    def out_norm(alpha_x, rows=slice(None)):
        o_ref[0, rows, :] = _layer_norm(alpha_x * xp_ref[0, rows, :] + o_ref[0, rows, :],
                                        row(ln_g_ref), row(ln_b_ref))

    @pl.when(g < last)
    def _():
        def take(lo, width):
            scr, off = proj_tile(lo)
            return scr[:, off:off + width]

        refillable = []
        refill_done = []

        def refill(*tiles):
            refillable.extend(tiles)
            while len(refillable) >= 2:
                pair = (refillable.pop(0), refillable.pop(0))
                w = jnp.concatenate([w_in_ref[:, t:t + D_G] for t in pair], axis=1)
                res = _dot(xb_scr[...], w)
                for i, t in enumerate(pair):
                    dst, dst_off = proj_tile(t)
                    dst[:, dst_off:dst_off + D_G] = res[:, i * D_G:(i + 1) * D_G]
                refill_done.append(res[tile - SUBLANES:, 2 * D_G - LANES:])

        def gate(lo):
            return jax.nn.silu(take(OFF_GATE + lo, D_G))

        xb_scr[...] = xn_ref[0].astype(BF16)

        out_matmul()

        pd = take(OFF_D, 2 * D_G)
        h_scr[HALO_D:HALO_D + tile, :] = pd[:, :D_G] * jax.nn.sigmoid(pd[:, D_G:])

        pa = take(OFF_A, 3 * D_G)
        xa, ba, ca = pa[:, :D_G], pa[:, D_G:2 * D_G], pa[:, 2 * D_G:]
        cx = ca * xa
        cx_scr[HALO_A:HALO_A + tile, :] = cx
        conv_a = conv_a_ref[2:3, :] * cx
        for k in range(CONV_A - 1):
            off = HALO_A - (CONV_A - 1) + k
            conv_a = conv_a + conv_a_ref[k:k + 1, :] * cx_scr[off:off + tile, :]
        h_a = (ba * conv_a * gate(0)).astype(BF16)
        cx_scr[0:HALO_A, :] = cx_scr[tile:tile + HALO_A, :]
        q = take(OFF_E, D_G).astype(BF16)

        lead = HALO_D - (CONV_D - 1)
        conv_d = []
        for c in range(tile // CONV_ROWS):
            base = c * CONV_ROWS
            bias = jnp.broadcast_to(row(dw_b_ref), (SUBLANES, D_G))
            if len(conv_d) >= CONV_CHAINS:
                bias = _sequenced_after(bias, conv_d[-CONV_CHAINS])
            acc = jnp.concatenate([bias] * (CONV_ROWS // SUBLANES), axis=0)
            for r in range(SUBLANES):
                rows = CONV_ROWS + (SUBLANES if r else 0)
                part = None
                for j in range(r, lead + CONV_D, SUBLANES):
                    if j < lead:
                        continue
                    term = dw_w_ref[j - lead:j - lead + 1, :] * h_scr[base + j - r:base + j - r + rows, :]
                    part = term if part is None else part + term
                acc = acc + part[r:r + CONV_ROWS, :]
            conv_d.append(acc)
        h_scr[0:HALO_D, :] = h_scr[tile:tile + HALO_D, :]
        refill(OFF_D, OFF_D + D_G, OFF_A, OFF_A + D_G)
        hn = jax.nn.silu(_layer_norm(jnp.concatenate(conv_d, axis=0), row(cc_g_ref), row(cc_b_ref)))
        y_d = _dot(hn.astype(BF16), pw_ref[...])
        scores = _dot(q, kbd_scr[...])
        refill(OFF_A + 2 * D_G, OFF_GATE)
        h_d = (y_d * gate(3 * D_G)).astype(BF16)

        mem_len = scores.shape[1] // N_SUB
        probs = []
        for h in range(N_SUB):
            sh = scores[:, h * mem_len:(h + 1) * mem_len]
            e = jnp.exp(sh - jnp.max(sh, axis=-1, keepdims=True))
            inv = 1.0 / jnp.sum(e, axis=-1, keepdims=True)
            probs.append((e * inv).astype(BF16))
        refill(OFF_GATE + 3 * D_G, OFF_E)
        y_e = _dot(jnp.concatenate(probs, axis=1), vbd_scr[...])

        pb = take(OFF_B, 2 * D_G)
        u = jax.nn.gelu(pb[:, :D_G])
        v = _layer_norm(jax.nn.gelu(pb[:, D_G:]), row(sg_g_ref), row(sg_b_ref))
        vb = v.astype(BF16)
        refill(OFF_B, OFF_B + D_G)
        wt = lax.broadcasted_iota(jnp.int32, (CHUNK, N_SUB * CHUNK), 0)
        ws = lax.broadcasted_iota(jnp.int32, (CHUNK, N_SUB * CHUNK), 1) % CHUNK
        w_mix = jnp.where(ws <= wt, sg_w_ref[...], 0.0).astype(BF16)
        lane_head = lax.broadcasted_iota(jnp.int32, (CHUNK, D_G), 1) // HEAD_DIM
        mixed = []
        for c in range(tile // CHUNK):
            vc = vb[c * CHUNK:(c + 1) * CHUNK, :]
            rhs = jnp.concatenate(
                [jnp.where(lane_head == h, vc, jnp.zeros_like(vc)) for h in range(N_SUB)], axis=0)
            mixed.append(_dot(w_mix, rhs) + sg_bias_ref[...])
        y_b = u * jnp.concatenate(mixed, axis=0)

        xc = take(OFF_C, D_G)
        xc_scr[HALO_C:HALO_C + tile, :] = xc

        xe = xc_scr[...]

        def doubled(val, steps):
            for shift in steps:
                val = val + pltpu.roll(val, shift, axis=0)
            return val

        s2 = doubled(xe[:, :2 * HEAD_DIM], (1,))
        s4 = doubled(s2, (2,))
        s8 = doubled(xe[:, 2 * HEAD_DIM:], (1, 2, 4))
        s16 = doubled(s8, (8,))
        lane = lax.broadcasted_iota(jnp.int32, s2.shape, 1)
        win_sum = jnp.concatenate([jnp.where(lane < HEAD_DIM, s2, s4),
                                   jnp.where(lane < HEAD_DIM, s8, s16)], axis=1)[HALO_C:, :]
        t_abs = s * tile + lax.broadcasted_iota(jnp.int32, (HALO_C, D_G), 0)
        window = jnp.left_shift(2, lax.broadcasted_iota(jnp.int32, (HALO_C, D_G), 1) // HEAD_DIM)
        count = jnp.minimum(t_abs + 1, window).astype(F32)
        inv_window = 1.0 / window[0:1, :].astype(F32)
        mean = jnp.concatenate([win_sum[:HALO_C, :] / count, win_sum[HALO_C:, :] * inv_window], axis=0)
        pooled = (mean - xc).astype(BF16)
        y_c = _dot(pooled, pool_w_ref[...]) * row(pool_scale_ref)
        xc_scr[0:HALO_C, :] = xc_scr[tile:tile + HALO_C, :]

        h_e = (y_e * gate(4 * D_G)).astype(BF16)
        refill(OFF_C, OFF_GATE + 4 * D_G)
        h_b = (y_b * gate(D_G)).astype(BF16)
        h_c = (y_c * gate(2 * D_G)).astype(BF16)
        refill(OFF_GATE + D_G, OFF_GATE + 2 * D_G)
        assert not refillable
        hcat_scr[...] = jnp.concatenate([h_a, h_b, h_c, h_d, h_e], axis=1)

        rows = tile // OUT_NORM_CHUNKS
        for c in range(OUT_NORM_CHUNKS):
            alpha1 = _sequenced_after(jnp.full((SUBLANES, LANES), alpha, F32), refill_done[OUT_NORM_FIRST_REFILL + c])
            alpha_t = jnp.concatenate([jnp.concatenate([alpha1] * (D_MODEL // LANES), axis=1)] * (rows // SUBLANES), axis=0)
            out_norm(alpha_t, slice(c * rows, (c + 1) * rows))

    @pl.when(g == last)
    def _():
        out_matmul()
        out_norm(alpha)


def _layer_spec(shape, layer):
    resident = dict(pipeline_mode=pl.Buffered(1))
    if len(shape) == 2:
        return pl.BlockSpec(tuple(shape), lambda g: (0, 0), **resident)
    return pl.BlockSpec((None,) + tuple(shape[1:]), lambda g: (layer,) + (0,) * (len(shape) - 1), **resident)


def _layer_call(x, mem, w_in, params, layer, alpha):
    batch, seq, _ = x.shape
    tile = SEQ_TILE
    assert seq % tile == 0 and tile % CHUNK == 0 and tile % CONV_ROWS == 0
    assert POOL_WINDOWS == tuple(2 << grp for grp in range(N_SUB)) and 2 * HEAD_DIM == LANES
    n_seq = seq // tile
    n_tiles = batch * n_seq
    mem_len = mem.shape[1]

    def tile_at(g):
        g = jnp.clip(g, 0, n_tiles - 1)
        return g // n_seq, g % n_seq

    in_specs = [
        pl.BlockSpec((1, tile, D_MODEL), lambda g: (*tile_at(g - 1), 0)),
        pl.BlockSpec((1, tile, D_MODEL), lambda g: (*tile_at(g + 1), 0)),
        pl.BlockSpec((1, mem_len, D_MODEL), lambda g: (tile_at(g)[0], 0, 0)),
        pl.BlockSpec(memory_space=pl.ANY),
    ] + [_layer_spec(p.shape, layer) for p in params]
    return pl.pallas_call(
        functools.partial(_layer_kernel, tile=tile, n_seq=n_seq, layer=layer, alpha=alpha),
        grid=(n_tiles + 1,),
        in_specs=in_specs,
        out_specs=pl.BlockSpec((1, tile, D_MODEL), lambda g: (*tile_at(g - 1), 0)),
        out_shape=jax.ShapeDtypeStruct(x.shape, x.dtype),
        scratch_shapes=[
            pltpu.VMEM((D_MODEL, D_IN), BF16),
            pltpu.VMEM((2, D_MODEL, W_STAGE_COLS), F32),
            pltpu.SemaphoreType.DMA((2,)),
            pltpu.VMEM((tile, D_MODEL), BF16),
            pltpu.VMEM((tile, D_MIX), BF16),
            pltpu.VMEM((D_G, N_SUB * mem_len), BF16),
            pltpu.VMEM((N_SUB * mem_len, D_G), BF16),
            pltpu.VMEM((HALO_A + tile, D_G), F32),
            pltpu.VMEM((HALO_C + tile, D_G), F32),
            pltpu.VMEM((HALO_D + tile, D_G), F32),
        ] + [pltpu.VMEM((tile, width), F32) for _, width in PROJ_GROUPS],
        compiler_params=pltpu.CompilerParams(
            dimension_semantics=("arbitrary",),
            vmem_limit_bytes=VMEM_LIMIT_BYTES),
        name="mixer_layer",
    )(x, x, mem, w_in, *params)


def kernel(x, mem, w_in, conv_a_w, sg_ln_g, sg_ln_b, sg_w, sg_b, pool_w, pool_scale, cc_dw_w, cc_dw_b, cc_ln_g, cc_ln_b, cc_pw_w, w_kv, w_out, ln_g, ln_b):
    depth = w_in.shape[0]
    alpha = (2.0 * depth) ** 0.25
    wkt = w_kv[:, :, :D_G].transpose(0, 2, 1).astype(BF16)
    wv = w_kv[:, :, D_G:].astype(BF16)
    sg_w_cat = sg_w.transpose(0, 2, 1, 3).reshape(depth, CHUNK, N_SUB * CHUNK)
    sg_bias = jnp.repeat(sg_b.transpose(0, 2, 1), HEAD_DIM, axis=2)
    pool_bd = jnp.einsum("gh,lgcd->lgchd", jnp.eye(N_SUB, dtype=pool_w.dtype), pool_w)
    pool_bd = pool_bd.reshape(depth, D_G, D_G).astype(BF16)
    params = (
        wkt, wv, conv_a_w, sg_ln_g, sg_ln_b, sg_w_cat, sg_bias,
        pool_bd, pool_scale, cc_dw_w, cc_dw_b, cc_ln_g, cc_ln_b,
        cc_pw_w.astype(BF16), w_out.transpose(0, 2, 1).astype(BF16), ln_g, ln_b,
    )
    for layer in range(depth):
        x = _layer_call(x, mem, w_in, params, layer, alpha)
    return x
```

```python
import functools

import jax
import jax.numpy as jnp
from jax import lax
from jax.experimental import pallas as pl
from jax.experimental.pallas import tpu as pltpu

D_MODEL = 1024
D_G = D_MODEL // 4
N_SUB = 4
HEAD_DIM = D_G // N_SUB
D_MIX = 5 * D_G
D_IN = 9 * D_G + D_MIX
CHUNK = 128
CONV_A = 3
CONV_D = 31
POOL_WINDOWS = (2, 4, 8, 16)
LN_EPS = 1e-5

SUBLANES = 8
LANES = 128

OFF_A = 0
OFF_B = 3 * D_G
OFF_C = 5 * D_G
OFF_D = 6 * D_G
OFF_E = 8 * D_G
OFF_GATE = 9 * D_G

HALO_A = 8
HALO_C = 16
HALO_D = 32
W_STAGE_COLS = 2 * D_G
WO_STAGE_ROWS = D_G
CONV_ROWS = 32
CONV_CHAINS = 2
OUT_NORM_CHUNKS = 4
OUT_NORM_FIRST_REFILL = 2

PROJ_GROUPS = ((OFF_A, 3 * D_G), (OFF_B, 2 * D_G), (OFF_C, D_G), (OFF_D, 2 * D_G), (OFF_E, D_G)) + tuple(
    (OFF_GATE + g * D_G, D_G) for g in range(5))

SEQ_TILE = 512
VMEM_LIMIT_BYTES = 56 * 1024 * 1024

F32 = jnp.float32
BF16 = jnp.bfloat16


def _dot(a, b):
    return jnp.dot(a, b, preferred_element_type=F32)


def _layer_norm(x, g, b):
    mu = jnp.mean(x, axis=-1, keepdims=True)
    xc = x - mu
    var = jnp.mean(xc * xc, axis=-1, keepdims=True)
    return xc * lax.rsqrt(var + LN_EPS) * g + b


def _sequenced_after(x, done):
    words = pltpu.bitcast(done, jnp.uint32)
    word = words[0:SUBLANES, 0:LANES]
    for i in range(0, words.shape[0], SUBLANES):
        for j in range(0, words.shape[1], LANES):
            if i or j:
                word = word + words[i:i + SUBLANES, j:j + LANES]
    word = jnp.max(word.astype(jnp.int32), axis=-1, keepdims=True).astype(jnp.uint32)
    zero = lax.shift_right_logical(lax.shift_right_logical(word, jnp.uint32(16)), jnp.uint32(16))
    return pltpu.bitcast(pltpu.bitcast(x, jnp.uint32) | zero, F32)


def _project_memory(mem_ref, wkt_ref, wv_ref, kbd_scr, vbd_scr):
    m = mem_ref[0].astype(BF16)
    mem_len = m.shape[0]
    kt = lax.dot_general(wkt_ref[...], m, (((1,), (1,)), ((), ())),
                         preferred_element_type=F32)
    kt = kt * (1.0 / (HEAD_DIM ** 0.5))
    v = _dot(m, wv_ref[...])
    row_head = lax.broadcasted_iota(jnp.int32, kt.shape, 0) // HEAD_DIM
    col_head = lax.broadcasted_iota(jnp.int32, v.shape, 1) // HEAD_DIM
    for h in range(N_SUB):
        kbd_scr[:, h * mem_len:(h + 1) * mem_len] = jnp.where(row_head == h, kt, 0.0).astype(BF16)
        vbd_scr[h * mem_len:(h + 1) * mem_len, :] = jnp.where(col_head == h, v, 0.0).astype(BF16)


def _layer_kernel(xp_ref, xn_ref, mem_ref, w_in_hbm, w_out_hbm, wkt_ref, wv_ref, conv_a_ref, sg_g_ref, sg_b_ref,
                  sg_w_ref, sg_bias_ref, pool_w_ref, pool_scale_ref, dw_w_ref, dw_b_ref, cc_g_ref, cc_b_ref,
                  pw_ref, ln_g_ref, ln_b_ref, o_ref, w_in_ref, w_stage, w_sem, w_out_ref, wo_stage, wo_sem,
                  xb_scr, hcat_scr, kbd_scr, vbd_scr,
                  cx_scr, xc_scr, h_scr, *proj_scrs, tile, n_seq, layer, alpha):
    g = pl.program_id(0)
    last = pl.num_programs(0) - 1
    s = g % n_seq
    proj_of = dict(zip(PROJ_GROUPS, proj_scrs))

    def row(ref):
        return ref[layer:layer + 1, :]

    def proj_tile(lo):
        group = [grp for grp in PROJ_GROUPS if grp[0] <= lo < grp[0] + grp[1]][0]
        return proj_of[group], lo - group[0]

    def w_in_copy(c):
        return pltpu.make_async_copy(
            w_in_hbm.at[layer, :, pl.ds(c * W_STAGE_COLS, W_STAGE_COLS)], w_stage.at[c % 2], w_sem.at[c % 2])

    def w_out_copy(c):
        return pltpu.make_async_copy(
            w_out_hbm.at[layer, pl.ds(c * WO_STAGE_ROWS, WO_STAGE_ROWS), :], wo_stage.at[c % 2], wo_sem.at[c % 2])

    @pl.when(g == 0)
    def _():
        n_chunks = D_IN // W_STAGE_COLS
        w_in_copy(0).start()
        xb0 = xp_ref[0].astype(BF16)
        for c in range(n_chunks + 1):
            if c + 1 < n_chunks:
                w_in_copy(c + 1).start()
            if c < n_chunks:
                w_in_copy(c).wait()
                w_in_ref[:, c * W_STAGE_COLS:(c + 1) * W_STAGE_COLS] = w_stage[c % 2].astype(BF16)
            if c:
                lo = (c - 1) * W_STAGE_COLS
                res = _dot(xb0, w_in_ref[:, lo:lo + W_STAGE_COLS])
                for t in range(0, W_STAGE_COLS, D_G):
                    dst, off = proj_tile(lo + t)
                    dst[:, off:off + D_G] = res[:, t:t + D_G]
        n_out = D_MIX // WO_STAGE_ROWS
        w_out_copy(0).start()
        for c in range(n_out):
            if c + 1 < n_out:
                w_out_copy(c + 1).start()
            w_out_copy(c).wait()
            w_out_ref[c * WO_STAGE_ROWS:(c + 1) * WO_STAGE_ROWS, :] = wo_stage[c % 2].astype(BF16)
        hcat_scr[...] = jnp.zeros(hcat_scr.shape, BF16)

    @pl.when(s == 0)
    def _():
        _project_memory(mem_ref, wkt_ref, wv_ref, kbd_scr, vbd_scr)
        cx_scr[0:HALO_A, :] = jnp.zeros((HALO_A, D_G), F32)
        xc_scr[0:HALO_C, :] = jnp.zeros((HALO_C, D_G), F32)
        h_scr[0:HALO_D, :] = jnp.zeros((HALO_D, D_G), F32)

    def out_matmul():
        o_ref[0] = _dot(hcat_scr[...], w_out_ref[...])

    def out_norm(alpha_x, rows=slice(None)):
        o_ref[0, rows, :] = _layer_norm(alpha_x * xp_ref[0, rows, :] + o_ref[0, rows, :],
                                        row(ln_g_ref), row(ln_b_ref))

    @pl.when(g < last)
    def _():
        def take(lo, width):
            scr, off = proj_tile(lo)
            return scr[:, off:off + width]

        refillable = []
        refill_done = []

        def refill(*tiles):
            refillable.extend(tiles)
            while len(refillable) >= 2:
                pair = (refillable.pop(0), refillable.pop(0))
                w = jnp.concatenate([w_in_ref[:, t:t + D_G] for t in pair], axis=1)
                res = _dot(xb_scr[...], w)
                for i, t in enumerate(pair):
                    dst, dst_off = proj_tile(t)
                    dst[:, dst_off:dst_off + D_G] = res[:, i * D_G:(i + 1) * D_G]
                refill_done.append(res[tile - SUBLANES:, 2 * D_G - LANES:])

        def gate(lo):
            return jax.nn.silu(take(OFF_GATE + lo, D_G))

        xb_scr[...] = xn_ref[0].astype(BF16)

        out_matmul()

        pd = take(OFF_D, 2 * D_G)
        h_scr[HALO_D:HALO_D + tile, :] = pd[:, :D_G] * jax.nn.sigmoid(pd[:, D_G:])

        pa = take(OFF_A, 3 * D_G)
        xa, ba, ca = pa[:, :D_G], pa[:, D_G:2 * D_G], pa[:, 2 * D_G:]
        cx = ca * xa
        cx_scr[HALO_A:HALO_A + tile, :] = cx
        conv_a = conv_a_ref[2:3, :] * cx
        for k in range(CONV_A - 1):
            off = HALO_A - (CONV_A - 1) + k
            conv_a = conv_a + conv_a_ref[k:k + 1, :] * cx_scr[off:off + tile, :]
        h_a = (ba * conv_a * gate(0)).astype(BF16)
        cx_scr[0:HALO_A, :] = cx_scr[tile:tile + HALO_A, :]
        q = take(OFF_E, D_G).astype(BF16)

        lead = HALO_D - (CONV_D - 1)
        conv_d = []
        for c in range(tile // CONV_ROWS):
            base = c * CONV_ROWS
            bias = jnp.broadcast_to(row(dw_b_ref), (SUBLANES, D_G))
            if len(conv_d) >= CONV_CHAINS:
                bias = _sequenced_after(bias, conv_d[-CONV_CHAINS])
            acc = jnp.concatenate([bias] * (CONV_ROWS // SUBLANES), axis=0)
            for r in range(SUBLANES):
                rows = CONV_ROWS + (SUBLANES if r else 0)
                part = None
                for j in range(r, lead + CONV_D, SUBLANES):
                    if j < lead:
                        continue
                    term = dw_w_ref[j - lead:j - lead + 1, :] * h_scr[base + j - r:base + j - r + rows, :]
                    part = term if part is None else part + term
                acc = acc + part[r:r + CONV_ROWS, :]
            conv_d.append(acc)
        h_scr[0:HALO_D, :] = h_scr[tile:tile + HALO_D, :]
        refill(OFF_D, OFF_D + D_G, OFF_A, OFF_A + D_G)
        hn = jax.nn.silu(_layer_norm(jnp.concatenate(conv_d, axis=0), row(cc_g_ref), row(cc_b_ref)))
        y_d = _dot(hn.astype(BF16), pw_ref[...])
        scores = _dot(q, kbd_scr[...])
        refill(OFF_A + 2 * D_G, OFF_GATE)
        h_d = (y_d * gate(3 * D_G)).astype(BF16)

        mem_len = scores.shape[1] // N_SUB
        probs = []
        for h in range(N_SUB):
            sh = scores[:, h * mem_len:(h + 1) * mem_len]
            e = jnp.exp(sh - jnp.max(sh, axis=-1, keepdims=True))
            inv = 1.0 / jnp.sum(e, axis=-1, keepdims=True)
            probs.append((e * inv).astype(BF16))
        refill(OFF_GATE + 3 * D_G, OFF_E)
        y_e = _dot(jnp.concatenate(probs, axis=1), vbd_scr[...])

        pb = take(OFF_B, 2 * D_G)
        u = jax.nn.gelu(pb[:, :D_G])
        v = _layer_norm(jax.nn.gelu(pb[:, D_G:]), row(sg_g_ref), row(sg_b_ref))
        vb = v.astype(BF16)
        refill(OFF_B, OFF_B + D_G)
        wt = lax.broadcasted_iota(jnp.int32, (CHUNK, N_SUB * CHUNK), 0)
        ws = lax.broadcasted_iota(jnp.int32, (CHUNK, N_SUB * CHUNK), 1) % CHUNK
        w_mix = jnp.where(ws <= wt, sg_w_ref[...], 0.0).astype(BF16)
        lane_head = lax.broadcasted_iota(jnp.int32, (CHUNK, D_G), 1) // HEAD_DIM
        mixed = []
        for c in range(tile // CHUNK):
            vc = vb[c * CHUNK:(c + 1) * CHUNK, :]
            rhs = jnp.concatenate(
                [jnp.where(lane_head == h, vc, jnp.zeros_like(vc)) for h in range(N_SUB)], axis=0)
            mixed.append(_dot(w_mix, rhs) + sg_bias_ref[...])
        y_b = u * jnp.concatenate(mixed, axis=0)

        xc = take(OFF_C, D_G)
        xc_scr[HALO_C:HALO_C + tile, :] = xc

        xe = xc_scr[...]

        def doubled(val, steps):
            for shift in steps:
                val = val + pltpu.roll(val, shift, axis=0)
            return val

        s2 = doubled(xe[:, :2 * HEAD_DIM], (1,))
        s4 = doubled(s2, (2,))
        s8 = doubled(xe[:, 2 * HEAD_DIM:], (1, 2, 4))
        s16 = doubled(s8, (8,))
        lane = lax.broadcasted_iota(jnp.int32, s2.shape, 1)
        win_sum = jnp.concatenate([jnp.where(lane < HEAD_DIM, s2, s4),
                                   jnp.where(lane < HEAD_DIM, s8, s16)], axis=1)[HALO_C:, :]
        t_abs = s * tile + lax.broadcasted_iota(jnp.int32, (HALO_C, D_G), 0)
        window = jnp.left_shift(2, lax.broadcasted_iota(jnp.int32, (HALO_C, D_G), 1) // HEAD_DIM)
        count = jnp.minimum(t_abs + 1, window).astype(F32)
        inv_window = 1.0 / window[0:1, :].astype(F32)
        mean = jnp.concatenate([win_sum[:HALO_C, :] / count, win_sum[HALO_C:, :] * inv_window], axis=0)
        pooled = (mean - xc).astype(BF16)
        y_c = _dot(pooled, pool_w_ref[...]) * row(pool_scale_ref)
        xc_scr[0:HALO_C, :] = xc_scr[tile:tile + HALO_C, :]

        h_e = (y_e * gate(4 * D_G)).astype(BF16)
        refill(OFF_C, OFF_GATE + 4 * D_G)
        h_b = (y_b * gate(D_G)).astype(BF16)
        h_c = (y_c * gate(2 * D_G)).astype(BF16)
        refill(OFF_GATE + D_G, OFF_GATE + 2 * D_G)
        assert not refillable
        hcat_scr[...] = jnp.concatenate([h_a, h_b, h_c, h_d, h_e], axis=1)

        rows = tile // OUT_NORM_CHUNKS
        for c in range(OUT_NORM_CHUNKS):
            alpha1 = _sequenced_after(jnp.full((SUBLANES, LANES), alpha, F32), refill_done[OUT_NORM_FIRST_REFILL + c])
            alpha_t = jnp.concatenate([jnp.concatenate([alpha1] * (D_MODEL // LANES), axis=1)] * (rows // SUBLANES), axis=0)
            out_norm(alpha_t, slice(c * rows, (c + 1) * rows))

    @pl.when(g == last)
    def _():
        out_matmul()
        out_norm(alpha)


def _layer_spec(shape, layer):
    resident = dict(pipeline_mode=pl.Buffered(1))
    if len(shape) == 2:
        return pl.BlockSpec(tuple(shape), lambda g: (0, 0), **resident)
    return pl.BlockSpec((None,) + tuple(shape[1:]), lambda g: (layer,) + (0,) * (len(shape) - 1), **resident)


def _layer_call(x, mem, w_in, w_out, params, layer, alpha):
    batch, seq, _ = x.shape
    tile = SEQ_TILE
    assert seq % tile == 0 and tile % CHUNK == 0 and tile % CONV_ROWS == 0
    assert POOL_WINDOWS == tuple(2 << grp for grp in range(N_SUB)) and 2 * HEAD_DIM == LANES
    n_seq = seq // tile
    n_tiles = batch * n_seq
    mem_len = mem.shape[1]

    def tile_at(g):
        g = jnp.clip(g, 0, n_tiles - 1)
        return g // n_seq, g % n_seq

    in_specs = [
        pl.BlockSpec((1, tile, D_MODEL), lambda g: (*tile_at(g - 1), 0)),
        pl.BlockSpec((1, tile, D_MODEL), lambda g: (*tile_at(g + 1), 0)),
        pl.BlockSpec((1, mem_len, D_MODEL), lambda g: (tile_at(g)[0], 0, 0)),
        pl.BlockSpec(memory_space=pl.ANY),
        pl.BlockSpec(memory_space=pl.ANY),
    ] + [_layer_spec(p.shape, layer) for p in params]
    return pl.pallas_call(
        functools.partial(_layer_kernel, tile=tile, n_seq=n_seq, layer=layer, alpha=alpha),
        grid=(n_tiles + 1,),
        in_specs=in_specs,
        out_specs=pl.BlockSpec((1, tile, D_MODEL), lambda g: (*tile_at(g - 1), 0)),
        out_shape=jax.ShapeDtypeStruct(x.shape, x.dtype),
        scratch_shapes=[
            pltpu.VMEM((D_MODEL, D_IN), BF16),
            pltpu.VMEM((2, D_MODEL, W_STAGE_COLS), F32),
            pltpu.SemaphoreType.DMA((2,)),
            pltpu.VMEM((D_MIX, D_MODEL), BF16),
            pltpu.VMEM((2, WO_STAGE_ROWS, D_MODEL), F32),
            pltpu.SemaphoreType.DMA((2,)),
            pltpu.VMEM((tile, D_MODEL), BF16),
            pltpu.VMEM((tile, D_MIX), BF16),
            pltpu.VMEM((D_G, N_SUB * mem_len), BF16),
            pltpu.VMEM((N_SUB * mem_len, D_G), BF16),
            pltpu.VMEM((HALO_A + tile, D_G), F32),
            pltpu.VMEM((HALO_C + tile, D_G), F32),
            pltpu.VMEM((HALO_D + tile, D_G), F32),
        ] + [pltpu.VMEM((tile, width), F32) for _, width in PROJ_GROUPS],
        compiler_params=pltpu.CompilerParams(
            dimension_semantics=("arbitrary",),
            vmem_limit_bytes=VMEM_LIMIT_BYTES),
        name="mixer_layer",
    )(x, x, mem, w_in, w_out, *params)


def kernel(x, mem, w_in, conv_a_w, sg_ln_g, sg_ln_b, sg_w, sg_b, pool_w, pool_scale, cc_dw_w, cc_dw_b, cc_ln_g, cc_ln_b, cc_pw_w, w_kv, w_out, ln_g, ln_b):
    depth = w_in.shape[0]
    alpha = (2.0 * depth) ** 0.25
    wkt = w_kv[:, :, :D_G].transpose(0, 2, 1).astype(BF16)
    wv = w_kv[:, :, D_G:].astype(BF16)
    sg_w_cat = sg_w.transpose(0, 2, 1, 3).reshape(depth, CHUNK, N_SUB * CHUNK)
    sg_bias = jnp.repeat(sg_b.transpose(0, 2, 1), HEAD_DIM, axis=2)
    pool_bd = jnp.einsum("gh,lgcd->lgchd", jnp.eye(N_SUB, dtype=pool_w.dtype), pool_w)
    pool_bd = pool_bd.reshape(depth, D_G, D_G).astype(BF16)
    params = (
        wkt, wv, conv_a_w, sg_ln_g, sg_ln_b, sg_w_cat, sg_bias,
        pool_bd, pool_scale, cc_dw_w, cc_dw_b, cc_ln_g, cc_ln_b,
        cc_pw_w.astype(BF16), ln_g, ln_b,
    )
    for layer in range(depth):
        x = _layer_call(x, mem, w_in, w_out, params, layer, alpha)
    return x
```

```python
import functools

import jax
import jax.numpy as jnp
from jax import lax
from jax.experimental import pallas as pl
from jax.experimental.pallas import tpu as pltpu

D_MODEL = 1024
D_G = D_MODEL // 4
N_SUB = 4
HEAD_DIM = D_G // N_SUB
D_MIX = 5 * D_G
D_IN = 9 * D_G + D_MIX
CHUNK = 128
CONV_A = 3
CONV_D = 31
POOL_WINDOWS = (2, 4, 8, 16)
LN_EPS = 1e-5

SUBLANES = 8
LANES = 128

OFF_A = 0
OFF_B = 3 * D_G
OFF_C = 5 * D_G
OFF_D = 6 * D_G
OFF_E = 8 * D_G
OFF_GATE = 9 * D_G

HALO_A = 8
HALO_C = 16
HALO_D = 32
W_STAGE_COLS = 2 * D_G
CONV_ROWS = 32
CONV_CHAINS = 2
OUT_NORM_CHUNKS = 4
OUT_NORM_FIRST_REFILL = 2

PROJ_GROUPS = ((OFF_A, 3 * D_G), (OFF_B, 2 * D_G), (OFF_C, D_G), (OFF_D, 2 * D_G), (OFF_E, D_G)) + tuple(
    (OFF_GATE + g * D_G, D_G) for g in range(5))

SEQ_TILE = 512
VMEM_LIMIT_BYTES = 56 * 1024 * 1024

F32 = jnp.float32
BF16 = jnp.bfloat16


def _dot(a, b):
    return jnp.dot(a, b, preferred_element_type=F32)


def _layer_norm(x, g, b):
    mu = jnp.mean(x, axis=-1, keepdims=True)
    xc = x - mu
    var = jnp.mean(xc * xc, axis=-1, keepdims=True)
    return xc * lax.rsqrt(var + LN_EPS) * g + b


def _sequenced_after(x, done):
    words = pltpu.bitcast(done, jnp.uint32)
    word = words[0:SUBLANES, 0:LANES]
    for i in range(0, words.shape[0], SUBLANES):
        for j in range(0, words.shape[1], LANES):
            if i or j:
                word = word + words[i:i + SUBLANES, j:j + LANES]
    word = jnp.max(word.astype(jnp.int32), axis=-1, keepdims=True).astype(jnp.uint32)
    zero = lax.shift_right_logical(lax.shift_right_logical(word, jnp.uint32(16)), jnp.uint32(16))
    return pltpu.bitcast(pltpu.bitcast(x, jnp.uint32) | zero, F32)


def _project_memory(mem_ref, wkt_ref, wv_ref, kbd_scr, vbd_scr):
    m = mem_ref[0].astype(BF16)
    mem_len = m.shape[0]
    kt = lax.dot_general(wkt_ref[...], m, (((1,), (1,)), ((), ())),
                         preferred_element_type=F32)
    kt = kt * (1.0 / (HEAD_DIM ** 0.5))
    v = _dot(m, wv_ref[...])
    row_head = lax.broadcasted_iota(jnp.int32, kt.shape, 0) // HEAD_DIM
    col_head = lax.broadcasted_iota(jnp.int32, v.shape, 1) // HEAD_DIM
    for h in range(N_SUB):
        kbd_scr[:, h * mem_len:(h + 1) * mem_len] = jnp.where(row_head == h, kt, 0.0).astype(BF16)
        vbd_scr[h * mem_len:(h + 1) * mem_len, :] = jnp.where(col_head == h, v, 0.0).astype(BF16)


def _layer_kernel(xp_ref, xn_ref, mem_ref, w_in_hbm, wkt_ref, wv_ref, conv_a_ref, sg_g_ref, sg_b_ref, sg_w_ref,
                  sg_bias_ref, pool_w_ref, pool_scale_ref, dw_w_ref, dw_b_ref, cc_g_ref, cc_b_ref,
                  pw_ref, w_out_ref, ln_g_ref, ln_b_ref, o_ref, w_in_ref, w_stage, w_sem, xb_scr, hcat_scr, kbd_scr, vbd_scr,
                  cx_scr, xc_scr, h_scr, *proj_scrs, tile, n_seq, layer, alpha):
    g = pl.program_id(0)
    last = pl.num_programs(0) - 1
    s = g % n_seq
    proj_of = dict(zip(PROJ_GROUPS, proj_scrs))

    def row(ref):
        return ref[layer:layer + 1, :]

    def proj_tile(lo):
        group = [grp for grp in PROJ_GROUPS if grp[0] <= lo < grp[0] + grp[1]][0]
        return proj_of[group], lo - group[0]

    def w_in_cols(los):
        return jnp.concatenate([w_in_ref[lo // D_G] for lo in los], axis=1)

    def w_in_copy(c):
        return pltpu.make_async_copy(
            w_in_hbm.at[layer, :, pl.ds(c * W_STAGE_COLS, W_STAGE_COLS)], w_stage.at[c % 2], w_sem.at[c % 2])

    @pl.when(g == 0)
    def _():
        n_chunks = D_IN // W_STAGE_COLS
        w_in_copy(0).start()
        xb0 = xp_ref[0].astype(BF16)
        for c in range(n_chunks + 1):
            if c + 1 < n_chunks:
                w_in_copy(c + 1).start()
            if c < n_chunks:
                w_in_copy(c).wait()
                for t in range(0, W_STAGE_COLS, D_G):
                    w_in_ref[(c * W_STAGE_COLS + t) // D_G] = w_stage[c % 2, :, t:t + D_G].astype(BF16)
            if c:
                lo = (c - 1) * W_STAGE_COLS
                res = _dot(xb0, w_in_cols(range(lo, lo + W_STAGE_COLS, D_G)))
                for t in range(0, W_STAGE_COLS, D_G):
                    dst, off = proj_tile(lo + t)
                    dst[:, off:off + D_G] = res[:, t:t + D_G]
        hcat_scr[...] = jnp.zeros(hcat_scr.shape, BF16)

    @pl.when(s == 0)
    def _():
        _project_memory(mem_ref, wkt_ref, wv_ref, kbd_scr, vbd_scr)
        cx_scr[0:HALO_A, :] = jnp.zeros((HALO_A, D_G), F32)
        xc_scr[0:HALO_C, :] = jnp.zeros((HALO_C, D_G), F32)
        h_scr[0:HALO_D, :] = jnp.zeros((HALO_D, D_G), F32)

    def out_matmul():
        o_ref[0] = lax.dot_general(hcat_scr[...], w_out_ref[...], (((1,), (1,)), ((), ())),
                                   preferred_element_type=F32)

    def out_norm(alpha_x, rows=slice(None)):
        o_ref[0, rows, :] = _layer_norm(alpha_x * xp_ref[0, rows, :] + o_ref[0, rows, :],
                                        row(ln_g_ref), row(ln_b_ref))

    @pl.when(g < last)
    def _():
        def take(lo, width):
            scr, off = proj_tile(lo)
            return scr[:, off:off + width]

        refillable = []
        refill_done = []

        def refill(*tiles):
            refillable.extend(tiles)
            while len(refillable) >= 2:
                pair = (refillable.pop(0), refillable.pop(0))
                res = _dot(xb_scr[...], w_in_cols(pair))
                for i, t in enumerate(pair):
                    dst, dst_off = proj_tile(t)
                    dst[:, dst_off:dst_off + D_G] = res[:, i * D_G:(i + 1) * D_G]
                refill_done.append(res[tile - SUBLANES:, 2 * D_G - LANES:])

        def gate(lo):
            return jax.nn.silu(take(OFF_GATE + lo, D_G))

        xb_scr[...] = xn_ref[0].astype(BF16)

        out_matmul()

        pd = take(OFF_D, 2 * D_G)
        h_scr[HALO_D:HALO_D + tile, :] = pd[:, :D_G] * jax.nn.sigmoid(pd[:, D_G:])

        pa = take(OFF_A, 3 * D_G)
        xa, ba, ca = pa[:, :D_G], pa[:, D_G:2 * D_G], pa[:, 2 * D_G:]
        cx = ca * xa
        cx_scr[HALO_A:HALO_A + tile, :] = cx
        conv_a = conv_a_ref[2:3, :] * cx
        for k in range(CONV_A - 1):
            off = HALO_A - (CONV_A - 1) + k
            conv_a = conv_a + conv_a_ref[k:k + 1, :] * cx_scr[off:off + tile, :]
        h_a = (ba * conv_a * gate(0)).astype(BF16)
        cx_scr[0:HALO_A, :] = cx_scr[tile:tile + HALO_A, :]
        q = take(OFF_E, D_G).astype(BF16)

        lead = HALO_D - (CONV_D - 1)
        conv_d = []
        for c in range(tile // CONV_ROWS):
            base = c * CONV_ROWS
            bias = jnp.broadcast_to(row(dw_b_ref), (SUBLANES, D_G))
            if len(conv_d) >= CONV_CHAINS:
                bias = _sequenced_after(bias, conv_d[-CONV_CHAINS])
            acc = jnp.concatenate([bias] * (CONV_ROWS // SUBLANES), axis=0)
            for r in range(SUBLANES):
                rows = CONV_ROWS + (SUBLANES if r else 0)
                part = None
                for j in range(r, lead + CONV_D, SUBLANES):
                    if j < lead:
                        continue
                    term = dw_w_ref[j - lead:j - lead + 1, :] * h_scr[base + j - r:base + j - r + rows, :]
                    part = term if part is None else part + term
                acc = acc + part[r:r + CONV_ROWS, :]
            conv_d.append(acc)
        h_scr[0:HALO_D, :] = h_scr[tile:tile + HALO_D, :]
        refill(OFF_D, OFF_D + D_G, OFF_A, OFF_A + D_G)
        hn = jax.nn.silu(_layer_norm(jnp.concatenate(conv_d, axis=0), row(cc_g_ref), row(cc_b_ref)))
        y_d = _dot(hn.astype(BF16), pw_ref[...])
        scores = _dot(q, kbd_scr[...])
        refill(OFF_A + 2 * D_G, OFF_GATE)
        h_d = (y_d * gate(3 * D_G)).astype(BF16)

        mem_len = scores.shape[1] // N_SUB
        probs = []
        for h in range(N_SUB):
            sh = scores[:, h * mem_len:(h + 1) * mem_len]
            e = jnp.exp(sh - jnp.max(sh, axis=-1, keepdims=True))
            inv = 1.0 / jnp.sum(e, axis=-1, keepdims=True)
            probs.append((e * inv).astype(BF16))
        refill(OFF_GATE + 3 * D_G, OFF_E)
        y_e = _dot(jnp.concatenate(probs, axis=1), vbd_scr[...])

        pb = take(OFF_B, 2 * D_G)
        u = jax.nn.gelu(pb[:, :D_G])
        v = _layer_norm(jax.nn.gelu(pb[:, D_G:]), row(sg_g_ref), row(sg_b_ref))
        vb = v.astype(BF16)
        refill(OFF_B, OFF_B + D_G)
        wt = lax.broadcasted_iota(jnp.int32, (CHUNK, N_SUB * CHUNK), 0)
        ws = lax.broadcasted_iota(jnp.int32, (CHUNK, N_SUB * CHUNK), 1) % CHUNK
        w_mix = jnp.where(ws <= wt, sg_w_ref[...], 0.0).astype(BF16)
        lane_head = lax.broadcasted_iota(jnp.int32, (CHUNK, D_G), 1) // HEAD_DIM
        mixed = []
        for c in range(tile // CHUNK):
            vc = vb[c * CHUNK:(c + 1) * CHUNK, :]
            rhs = jnp.concatenate(
                [jnp.where(lane_head == h, vc, jnp.zeros_like(vc)) for h in range(N_SUB)], axis=0)
            mixed.append(_dot(w_mix, rhs) + sg_bias_ref[...])
        y_b = u * jnp.concatenate(mixed, axis=0)

        xc = take(OFF_C, D_G)
        xc_scr[HALO_C:HALO_C + tile, :] = xc

        xe = xc_scr[...]

        def doubled(val, steps):
            for shift in steps:
                val = val + pltpu.roll(val, shift, axis=0)
            return val

        s2 = doubled(xe[:, :2 * HEAD_DIM], (1,))
        s4 = doubled(s2, (2,))
        s8 = doubled(xe[:, 2 * HEAD_DIM:], (1, 2, 4))
        s16 = doubled(s8, (8,))
        lane = lax.broadcasted_iota(jnp.int32, s2.shape, 1)
        win_sum = jnp.concatenate([jnp.where(lane < HEAD_DIM, s2, s4),
                                   jnp.where(lane < HEAD_DIM, s8, s16)], axis=1)[HALO_C:, :]
        t_abs = s * tile + lax.broadcasted_iota(jnp.int32, (HALO_C, D_G), 0)
        window = jnp.left_shift(2, lax.broadcasted_iota(jnp.int32, (HALO_C, D_G), 1) // HEAD_DIM)
        count = jnp.minimum(t_abs + 1, window).astype(F32)
        inv_window = 1.0 / window[0:1, :].astype(F32)
        mean = jnp.concatenate([win_sum[:HALO_C, :] / count, win_sum[HALO_C:, :] * inv_window], axis=0)
        pooled = (mean - xc).astype(BF16)
        y_c = _dot(pooled, pool_w_ref[...]) * row(pool_scale_ref)
        xc_scr[0:HALO_C, :] = xc_scr[tile:tile + HALO_C, :]

        h_e = (y_e * gate(4 * D_G)).astype(BF16)
        refill(OFF_C, OFF_GATE + 4 * D_G)
        h_b = (y_b * gate(D_G)).astype(BF16)
        h_c = (y_c * gate(2 * D_G)).astype(BF16)
        refill(OFF_GATE + D_G, OFF_GATE + 2 * D_G)
        assert not refillable
        hcat_scr[...] = jnp.concatenate([h_a, h_b, h_c, h_d, h_e], axis=1)

        rows = tile // OUT_NORM_CHUNKS
        for c in range(OUT_NORM_CHUNKS):
            alpha1 = _sequenced_after(jnp.full((SUBLANES, LANES), alpha, F32), refill_done[OUT_NORM_FIRST_REFILL + c])
            alpha_t = jnp.concatenate([jnp.concatenate([alpha1] * (D_MODEL // LANES), axis=1)] * (rows // SUBLANES), axis=0)
            out_norm(alpha_t, slice(c * rows, (c + 1) * rows))

    @pl.when(g == last)
    def _():
        out_matmul()
        out_norm(alpha)


def _layer_spec(shape, layer):
    resident = dict(pipeline_mode=pl.Buffered(1))
    if len(shape) == 2:
        return pl.BlockSpec(tuple(shape), lambda g: (0, 0), **resident)
    return pl.BlockSpec((None,) + tuple(shape[1:]), lambda g: (layer,) + (0,) * (len(shape) - 1), **resident)


def _layer_call(x, mem, w_in, params, layer, alpha):
    batch, seq, _ = x.shape
    tile = SEQ_TILE
    assert seq % tile == 0 and tile % CHUNK == 0 and tile % CONV_ROWS == 0
    assert POOL_WINDOWS == tuple(2 << grp for grp in range(N_SUB)) and 2 * HEAD_DIM == LANES
    n_seq = seq // tile
    n_tiles = batch * n_seq
    mem_len = mem.shape[1]

    def tile_at(g):
        g = jnp.clip(g, 0, n_tiles - 1)
        return g // n_seq, g % n_seq

    in_specs = [
        pl.BlockSpec((1, tile, D_MODEL), lambda g: (*tile_at(g - 1), 0)),
        pl.BlockSpec((1, tile, D_MODEL), lambda g: (*tile_at(g + 1), 0)),
        pl.BlockSpec((1, mem_len, D_MODEL), lambda g: (tile_at(g)[0], 0, 0)),
        pl.BlockSpec(memory_space=pl.ANY),
    ] + [_layer_spec(p.shape, layer) for p in params]
    return pl.pallas_call(
        functools.partial(_layer_kernel, tile=tile, n_seq=n_seq, layer=layer, alpha=alpha),
        grid=(n_tiles + 1,),
        in_specs=in_specs,
        out_specs=pl.BlockSpec((1, tile, D_MODEL), lambda g: (*tile_at(g - 1), 0)),
        out_shape=jax.ShapeDtypeStruct(x.shape, x.dtype),
        scratch_shapes=[
            pltpu.VMEM((D_IN // D_G, D_MODEL, D_G), BF16),
            pltpu.VMEM((2, D_MODEL, W_STAGE_COLS), F32),
            pltpu.SemaphoreType.DMA((2,)),
            pltpu.VMEM((tile, D_MODEL), BF16),
            pltpu.VMEM((tile, D_MIX), BF16),
            pltpu.VMEM((D_G, N_SUB * mem_len), BF16),
            pltpu.VMEM((N_SUB * mem_len, D_G), BF16),
            pltpu.VMEM((HALO_A + tile, D_G), F32),
            pltpu.VMEM((HALO_C + tile, D_G), F32),
            pltpu.VMEM((HALO_D + tile, D_G), F32),
        ] + [pltpu.VMEM((tile, width), F32) for _, width in PROJ_GROUPS],
        compiler_params=pltpu.CompilerParams(
            dimension_semantics=("arbitrary",),
            vmem_limit_bytes=VMEM_LIMIT_BYTES),
        name="mixer_layer",
    )(x, x, mem, w_in, *params)


def kernel(x, mem, w_in, conv_a_w, sg_ln_g, sg_ln_b, sg_w, sg_b, pool_w, pool_scale, cc_dw_w, cc_dw_b, cc_ln_g, cc_ln_b, cc_pw_w, w_kv, w_out, ln_g, ln_b):
    depth = w_in.shape[0]
    alpha = (2.0 * depth) ** 0.25
    wkt = w_kv[:, :, :D_G].transpose(0, 2, 1).astype(BF16)
    wv = w_kv[:, :, D_G:].astype(BF16)
    sg_w_cat = sg_w.transpose(0, 2, 1, 3).reshape(depth, CHUNK, N_SUB * CHUNK)
    sg_bias = jnp.repeat(sg_b.transpose(0, 2, 1), HEAD_DIM, axis=2)
    pool_bd = jnp.einsum("gh,lgcd->lgchd", jnp.eye(N_SUB, dtype=pool_w.dtype), pool_w)
    pool_bd = pool_bd.reshape(depth, D_G, D_G).astype(BF16)
    params = (
        wkt, wv, conv_a_w, sg_ln_g, sg_ln_b, sg_w_cat, sg_bias,
        pool_bd, pool_scale, cc_dw_w, cc_dw_b, cc_ln_g, cc_ln_b,
        cc_pw_w.astype(BF16), w_out.transpose(0, 2, 1).astype(BF16), ln_g, ln_b,
    )
    for layer in range(depth):
        x = _layer_call(x, mem, w_in, params, layer, alpha)
    return x
```

```python
import functools

import jax
import jax.numpy as jnp
from jax import lax
from jax.experimental import pallas as pl
from jax.experimental.pallas import tpu as pltpu

D_MODEL = 1024
D_G = D_MODEL // 4
N_SUB = 4
HEAD_DIM = D_G // N_SUB
D_MIX = 5 * D_G
D_IN = 9 * D_G + D_MIX
CHUNK = 128
CONV_A = 3
CONV_D = 31
POOL_WINDOWS = (2, 4, 8, 16)
LN_EPS = 1e-5

SUBLANES = 8
LANES = 128

OFF_A = 0
OFF_B = 3 * D_G
OFF_C = 5 * D_G
OFF_D = 6 * D_G
OFF_E = 8 * D_G
OFF_GATE = 9 * D_G

HALO_A = 16
HALO_C = 16
HALO_D = 32
W_STAGE_COLS = 2 * D_G
CONV_ROWS = 32
CONV_CHAINS = 2
OUT_NORM_CHUNKS = 4
OUT_NORM_FIRST_REFILL = 2

PROJ_GROUPS = ((OFF_A, 3 * D_G), (OFF_B, 2 * D_G), (OFF_C, D_G), (OFF_D, 2 * D_G), (OFF_E, D_G)) + tuple(
    (OFF_GATE + g * D_G, D_G) for g in range(5))

SEQ_TILE = 512
VMEM_LIMIT_BYTES = 56 * 1024 * 1024

F32 = jnp.float32
BF16 = jnp.bfloat16


def _dot(a, b):
    return jnp.dot(a, b, preferred_element_type=F32)


def _layer_norm(x, g, b):
    mu = jnp.mean(x, axis=-1, keepdims=True)
    xc = x - mu
    var = jnp.mean(xc * xc, axis=-1, keepdims=True)
    return xc * lax.rsqrt(var + LN_EPS) * g + b


def _sequenced_after(x, done):
    words = pltpu.bitcast(done, jnp.uint32)
    word = words[0:SUBLANES, 0:LANES]
    for i in range(0, words.shape[0], SUBLANES):
        for j in range(0, words.shape[1], LANES):
            if i or j:
                word = word + words[i:i + SUBLANES, j:j + LANES]
    word = jnp.max(word.astype(jnp.int32), axis=-1, keepdims=True).astype(jnp.uint32)
    zero = lax.shift_right_logical(lax.shift_right_logical(word, jnp.uint32(16)), jnp.uint32(16))
    return pltpu.bitcast(pltpu.bitcast(x, jnp.uint32) | zero, F32)


def _project_memory(mem_ref, wkt_ref, wv_ref, kbd_scr, vbd_scr):
    m = mem_ref[0].astype(BF16)
    mem_len = m.shape[0]
    kt = lax.dot_general(wkt_ref[...], m, (((1,), (1,)), ((), ())),
                         preferred_element_type=F32)
    kt = kt * (1.0 / (HEAD_DIM ** 0.5))
    v = _dot(m, wv_ref[...])
    row_head = lax.broadcasted_iota(jnp.int32, kt.shape, 0) // HEAD_DIM
    col_head = lax.broadcasted_iota(jnp.int32, v.shape, 1) // HEAD_DIM
    for h in range(N_SUB):
        kbd_scr[:, h * mem_len:(h + 1) * mem_len] = jnp.where(row_head == h, kt, 0.0).astype(BF16)
        vbd_scr[h * mem_len:(h + 1) * mem_len, :] = jnp.where(col_head == h, v, 0.0).astype(BF16)


def _layer_kernel(xp_ref, xn_ref, mem_ref, w_in_hbm, wkt_ref, wv_ref, conv_a_ref, sg_g_ref, sg_b_ref, sg_w_ref,
                  sg_bias_ref, pool_w_ref, pool_scale_ref, dw_w_ref, dw_b_ref, cc_g_ref, cc_b_ref,
                  pw_ref, w_out_ref, ln_g_ref, ln_b_ref, o_ref, w_in_ref, w_stage, w_sem, xb_scr, hcat_scr, kbd_scr, vbd_scr,
                  cx_scr, xc_scr, h_scr, *proj_scrs, tile, n_seq, layer, alpha):
    g = pl.program_id(0)
    last = pl.num_programs(0) - 1
    s = g % n_seq
    proj_of = dict(zip(PROJ_GROUPS, proj_scrs))

    def row(ref):
        return ref[layer:layer + 1, :]

    def proj_tile(lo):
        group = [grp for grp in PROJ_GROUPS if grp[0] <= lo < grp[0] + grp[1]][0]
        return proj_of[group], lo - group[0]

    def w_in_copy(c):
        return pltpu.make_async_copy(
            w_in_hbm.at[layer, :, pl.ds(c * W_STAGE_COLS, W_STAGE_COLS)], w_stage.at[c % 2], w_sem.at[c % 2])

    @pl.when(g == 0)
    def _():
        n_chunks = D_IN // W_STAGE_COLS
        w_in_copy(0).start()
        xb0 = xp_ref[0].astype(BF16)
        for c in range(n_chunks + 1):
            if c + 1 < n_chunks:
                w_in_copy(c + 1).start()
            if c < n_chunks:
                w_in_copy(c).wait()
                w_in_ref[:, c * W_STAGE_COLS:(c + 1) * W_STAGE_COLS] = w_stage[c % 2].astype(BF16)
            if c:
                lo = (c - 1) * W_STAGE_COLS
                res = _dot(xb0, w_in_ref[:, lo:lo + W_STAGE_COLS])
                for t in range(0, W_STAGE_COLS, D_G):
                    dst, off = proj_tile(lo + t)
                    dst[:, off:off + D_G] = res[:, t:t + D_G]
        hcat_scr[...] = jnp.zeros(hcat_scr.shape, BF16)

    @pl.when(s == 0)
    def _():
        _project_memory(mem_ref, wkt_ref, wv_ref, kbd_scr, vbd_scr)
        cx_scr[0:HALO_A, :] = jnp.zeros((HALO_A, D_G), F32)
        xc_scr[0:HALO_C, :] = jnp.zeros((HALO_C, D_G), F32)
        h_scr[0:HALO_D, :] = jnp.zeros((HALO_D, D_G), F32)

    def out_matmul():
        o_ref[0] = lax.dot_general(hcat_scr[...], w_out_ref[...], (((1,), (1,)), ((), ())),
                                   preferred_element_type=F32)

    def out_norm(alpha_x, rows=slice(None)):
        o_ref[0, rows, :] = _layer_norm(alpha_x * xp_ref[0, rows, :] + o_ref[0, rows, :],
                                        row(ln_g_ref), row(ln_b_ref))

    @pl.when(g < last)
    def _():
        def take(lo, width):
            scr, off = proj_tile(lo)
            return scr[:, off:off + width]

        refillable = []
        refill_done = []

        def refill(*tiles):
            refillable.extend(tiles)
            while len(refillable) >= 2:
                pair = (refillable.pop(0), refillable.pop(0))
                w = jnp.concatenate([w_in_ref[:, t:t + D_G] for t in pair], axis=1)
                res = _dot(xb_scr[...], w)
                for i, t in enumerate(pair):
                    dst, dst_off = proj_tile(t)
                    dst[:, dst_off:dst_off + D_G] = res[:, i * D_G:(i + 1) * D_G]
                refill_done.append(res[tile - SUBLANES:, 2 * D_G - LANES:])

        def gate(lo):
            return jax.nn.silu(take(OFF_GATE + lo, D_G))

        xb_scr[...] = xn_ref[0].astype(BF16)

        out_matmul()

        pd = take(OFF_D, 2 * D_G)
        h_scr[HALO_D:HALO_D + tile, :] = pd[:, :D_G] * jax.nn.sigmoid(pd[:, D_G:])

        pa = take(OFF_A, 3 * D_G)
        xa, ba, ca = pa[:, :D_G], pa[:, D_G:2 * D_G], pa[:, 2 * D_G:]
        cx = ca * xa
        cx_scr[HALO_A:HALO_A + tile, :] = cx
        conv_a = conv_a_ref[2:3, :] * cx
        for k in range(CONV_A - 1):
            off = HALO_A - (CONV_A - 1) + k
            conv_a = conv_a + conv_a_ref[k:k + 1, :] * cx_scr[off:off + tile, :]
        h_a = (ba * conv_a * gate(0)).astype(BF16)
        cx_scr[0:HALO_A, :] = cx_scr[tile:tile + HALO_A, :]
        q = take(OFF_E, D_G).astype(BF16)

        lead = HALO_D - (CONV_D - 1)
        conv_d = []
        for c in range(tile // CONV_ROWS):
            base = c * CONV_ROWS
            bias = jnp.broadcast_to(row(dw_b_ref), (SUBLANES, D_G))
            if len(conv_d) >= CONV_CHAINS:
                bias = _sequenced_after(bias, conv_d[-CONV_CHAINS])
            acc = jnp.concatenate([bias] * (CONV_ROWS // SUBLANES), axis=0)
            for r in range(SUBLANES):
                rows = CONV_ROWS + (SUBLANES if r else 0)
                part = None
                for j in range(r, lead + CONV_D, SUBLANES):
                    if j < lead:
                        continue
                    term = dw_w_ref[j - lead:j - lead + 1, :] * h_scr[base + j - r:base + j - r + rows, :]
                    part = term if part is None else part + term
                acc = acc + part[r:r + CONV_ROWS, :]
            conv_d.append(acc)
        h_scr[0:HALO_D, :] = h_scr[tile:tile + HALO_D, :]
        refill(OFF_D, OFF_D + D_G, OFF_A, OFF_A + D_G)
        hn = jax.nn.silu(_layer_norm(jnp.concatenate(conv_d, axis=0), row(cc_g_ref), row(cc_b_ref)))
        y_d = _dot(hn.astype(BF16), pw_ref[...])
        scores = _dot(q, kbd_scr[...])
        refill(OFF_A + 2 * D_G, OFF_GATE)
        h_d = (y_d * gate(3 * D_G)).astype(BF16)

        mem_len = scores.shape[1] // N_SUB
        probs = []
        for h in range(N_SUB):
            sh = scores[:, h * mem_len:(h + 1) * mem_len]
            e = jnp.exp(sh - jnp.max(sh, axis=-1, keepdims=True))
            inv = 1.0 / jnp.sum(e, axis=-1, keepdims=True)
            probs.append((e * inv).astype(BF16))
        refill(OFF_GATE + 3 * D_G, OFF_E)
        y_e = _dot(jnp.concatenate(probs, axis=1), vbd_scr[...])

        pb = take(OFF_B, 2 * D_G)
        u = jax.nn.gelu(pb[:, :D_G])
        v = _layer_norm(jax.nn.gelu(pb[:, D_G:]), row(sg_g_ref), row(sg_b_ref))
        vb = v.astype(BF16)
        refill(OFF_B, OFF_B + D_G)
        wt = lax.broadcasted_iota(jnp.int32, (CHUNK, N_SUB * CHUNK), 0)
        ws = lax.broadcasted_iota(jnp.int32, (CHUNK, N_SUB * CHUNK), 1) % CHUNK
        w_mix = jnp.where(ws <= wt, sg_w_ref[...], 0.0).astype(BF16)
        lane_head = lax.broadcasted_iota(jnp.int32, (CHUNK, D_G), 1) // HEAD_DIM
        mixed = []
        for c in range(tile // CHUNK):
            vc = vb[c * CHUNK:(c + 1) * CHUNK, :]
            rhs = jnp.concatenate(
                [jnp.where(lane_head == h, vc, jnp.zeros_like(vc)) for h in range(N_SUB)], axis=0)
            mixed.append(_dot(w_mix, rhs) + sg_bias_ref[...])
        y_b = u * jnp.concatenate(mixed, axis=0)

        xc = take(OFF_C, D_G)
        xc_scr[HALO_C:HALO_C + tile, :] = xc

        xe = xc_scr[...]

        def doubled(val, steps):
            for shift in steps:
                val = val + pltpu.roll(val, shift, axis=0)
            return val

        s2 = doubled(xe[:, :2 * HEAD_DIM], (1,))
        s4 = doubled(s2, (2,))
        s8 = doubled(xe[:, 2 * HEAD_DIM:], (1, 2, 4))
        s16 = doubled(s8, (8,))
        lane = lax.broadcasted_iota(jnp.int32, s2.shape, 1)
        win_sum = jnp.concatenate([jnp.where(lane < HEAD_DIM, s2, s4),
                                   jnp.where(lane < HEAD_DIM, s8, s16)], axis=1)[HALO_C:, :]
        t_abs = s * tile + lax.broadcasted_iota(jnp.int32, (HALO_C, D_G), 0)
        window = jnp.left_shift(2, lax.broadcasted_iota(jnp.int32, (HALO_C, D_G), 1) // HEAD_DIM)
        count = jnp.minimum(t_abs + 1, window).astype(F32)
        inv_window = 1.0 / window[0:1, :].astype(F32)
        mean = jnp.concatenate([win_sum[:HALO_C, :] / count, win_sum[HALO_C:, :] * inv_window], axis=0)
        pooled = (mean - xc).astype(BF16)
        y_c = _dot(pooled, pool_w_ref[...]) * row(pool_scale_ref)
        xc_scr[0:HALO_C, :] = xc_scr[tile:tile + HALO_C, :]

        h_e = (y_e * gate(4 * D_G)).astype(BF16)
        refill(OFF_C, OFF_GATE + 4 * D_G)
        h_b = (y_b * gate(D_G)).astype(BF16)
        h_c = (y_c * gate(2 * D_G)).astype(BF16)
        refill(OFF_GATE + D_G, OFF_GATE + 2 * D_G)
        assert not refillable
        hcat_scr[...] = jnp.concatenate([h_a, h_b, h_c, h_d, h_e], axis=1)

        rows = tile // OUT_NORM_CHUNKS
        for c in range(OUT_NORM_CHUNKS):
            alpha1 = _sequenced_after(jnp.full((SUBLANES, LANES), alpha, F32), refill_done[OUT_NORM_FIRST_REFILL + c])
            alpha_t = jnp.concatenate([jnp.concatenate([alpha1] * (D_MODEL // LANES), axis=1)] * (rows // SUBLANES), axis=0)
            out_norm(alpha_t, slice(c * rows, (c + 1) * rows))

    @pl.when(g == last)
    def _():
        out_matmul()
        out_norm(alpha)


def _layer_spec(shape, layer):
    resident = dict(pipeline_mode=pl.Buffered(1))
    if len(shape) == 2:
        return pl.BlockSpec(tuple(shape), lambda g: (0, 0), **resident)
    return pl.BlockSpec((None,) + tuple(shape[1:]), lambda g: (layer,) + (0,) * (len(shape) - 1), **resident)


def _layer_call(x, mem, w_in, params, layer, alpha):
    batch, seq, _ = x.shape
    tile = SEQ_TILE
    assert seq % tile == 0 and tile % CHUNK == 0 and tile % CONV_ROWS == 0
    assert POOL_WINDOWS == tuple(2 << grp for grp in range(N_SUB)) and 2 * HEAD_DIM == LANES
    n_seq = seq // tile
    n_tiles = batch * n_seq
    mem_len = mem.shape[1]

    def tile_at(g):
        g = jnp.clip(g, 0, n_tiles - 1)
        return g // n_seq, g % n_seq

    in_specs = [
        pl.BlockSpec((1, tile, D_MODEL), lambda g: (*tile_at(g - 1), 0)),
        pl.BlockSpec((1, tile, D_MODEL), lambda g: (*tile_at(g + 1), 0)),
        pl.BlockSpec((1, mem_len, D_MODEL), lambda g: (tile_at(g)[0], 0, 0)),
        pl.BlockSpec(memory_space=pl.ANY),
    ] + [_layer_spec(p.shape, layer) for p in params]
    return pl.pallas_call(
        functools.partial(_layer_kernel, tile=tile, n_seq=n_seq, layer=layer, alpha=alpha),
        grid=(n_tiles + 1,),
        in_specs=in_specs,
        out_specs=pl.BlockSpec((1, tile, D_MODEL), lambda g: (*tile_at(g - 1), 0)),
        out_shape=jax.ShapeDtypeStruct(x.shape, x.dtype),
        scratch_shapes=[
            pltpu.VMEM((D_MODEL, D_IN), BF16),
            pltpu.VMEM((2, D_MODEL, W_STAGE_COLS), F32),
            pltpu.SemaphoreType.DMA((2,)),
            pltpu.VMEM((tile, D_MODEL), BF16),
            pltpu.VMEM((tile, D_MIX), BF16),
            pltpu.VMEM((D_G, N_SUB * mem_len), BF16),
            pltpu.VMEM((N_SUB * mem_len, D_G), BF16),
            pltpu.VMEM((HALO_A + tile, D_G), F32),
            pltpu.VMEM((HALO_C + tile, D_G), F32),
            pltpu.VMEM((HALO_D + tile, D_G), F32),
        ] + [pltpu.VMEM((tile, width), F32) for _, width in PROJ_GROUPS],
        compiler_params=pltpu.CompilerParams(
            dimension_semantics=("arbitrary",),
            vmem_limit_bytes=VMEM_LIMIT_BYTES),
        name="mixer_layer",
    )(x, x, mem, w_in, *params)


def kernel(x, mem, w_in, conv_a_w, sg_ln_g, sg_ln_b, sg_w, sg_b, pool_w, pool_scale, cc_dw_w, cc_dw_b, cc_ln_g, cc_ln_b, cc_pw_w, w_kv, w_out, ln_g, ln_b):
    depth = w_in.shape[0]
    alpha = (2.0 * depth) ** 0.25
    wkt = w_kv[:, :, :D_G].transpose(0, 2, 1).astype(BF16)
    wv = w_kv[:, :, D_G:].astype(BF16)
    sg_w_cat = sg_w.transpose(0, 2, 1, 3).reshape(depth, CHUNK, N_SUB * CHUNK)
    sg_bias = jnp.repeat(sg_b.transpose(0, 2, 1), HEAD_DIM, axis=2)
    pool_bd = jnp.einsum("gh,lgcd->lgchd", jnp.eye(N_SUB, dtype=pool_w.dtype), pool_w)
    pool_bd = pool_bd.reshape(depth, D_G, D_G).astype(BF16)
    params = (
        wkt, wv, conv_a_w, sg_ln_g, sg_ln_b, sg_w_cat, sg_bias,
        pool_bd, pool_scale, cc_dw_w, cc_dw_b, cc_ln_g, cc_ln_b,
        cc_pw_w.astype(BF16), w_out.transpose(0, 2, 1).astype(BF16), ln_g, ln_b,
    )
    for layer in range(depth):
        x = _layer_call(x, mem, w_in, params, layer, alpha)
    return x
```

```python
import functools

import jax
import jax.numpy as jnp
from jax import lax
from jax.experimental import pallas as pl
from jax.experimental.pallas import tpu as pltpu

D_MODEL = 1024
D_G = D_MODEL // 4
N_SUB = 4
HEAD_DIM = D_G // N_SUB
D_MIX = 5 * D_G
D_IN = 9 * D_G + D_MIX
CHUNK = 128
CONV_A = 3
CONV_D = 31
POOL_WINDOWS = (2, 4, 8, 16)
LN_EPS = 1e-5

SUBLANES = 8
LANES = 128

OFF_A = 0
OFF_B = 3 * D_G
OFF_C = 5 * D_G
OFF_D = 6 * D_G
OFF_E = 8 * D_G
OFF_GATE = 9 * D_G

HALO_A = 8
HALO_C = 16
HALO_D = 32
X_RING = 3
W_STAGE_COLS = 2 * D_G
CONV_ROWS = 32
CONV_CHAINS = 2
OUT_NORM_CHUNKS = 4
OUT_NORM_FIRST_REFILL = 2

PROJ_GROUPS = ((OFF_A, 3 * D_G), (OFF_B, 2 * D_G), (OFF_C, D_G), (OFF_D, 2 * D_G), (OFF_E, D_G)) + tuple(
    (OFF_GATE + g * D_G, D_G) for g in range(5))

SEQ_TILE = 512
VMEM_LIMIT_BYTES = 56 * 1024 * 1024

F32 = jnp.float32
BF16 = jnp.bfloat16


def _dot(a, b):
    return jnp.dot(a, b, preferred_element_type=F32)


def _layer_norm(x, g, b):
    mu = jnp.mean(x, axis=-1, keepdims=True)
    xc = x - mu
    var = jnp.mean(xc * xc, axis=-1, keepdims=True)
    return xc * lax.rsqrt(var + LN_EPS) * g + b


def _sequenced_after(x, done):
    words = pltpu.bitcast(done, jnp.uint32)
    word = words[0:SUBLANES, 0:LANES]
    for i in range(0, words.shape[0], SUBLANES):
        for j in range(0, words.shape[1], LANES):
            if i or j:
                word = word + words[i:i + SUBLANES, j:j + LANES]
    word = jnp.max(word.astype(jnp.int32), axis=-1, keepdims=True).astype(jnp.uint32)
    zero = lax.shift_right_logical(lax.shift_right_logical(word, jnp.uint32(16)), jnp.uint32(16))
    return pltpu.bitcast(pltpu.bitcast(x, jnp.uint32) | zero, F32)


def _project_memory(mem_ref, wkt_ref, wv_ref, kbd_scr, vbd_scr):
    m = mem_ref[0].astype(BF16)
    mem_len = m.shape[0]
    kt = lax.dot_general(wkt_ref[...], m, (((1,), (1,)), ((), ())),
                         preferred_element_type=F32)
    kt = kt * (1.0 / (HEAD_DIM ** 0.5))
    v = _dot(m, wv_ref[...])
    row_head = lax.broadcasted_iota(jnp.int32, kt.shape, 0) // HEAD_DIM
    col_head = lax.broadcasted_iota(jnp.int32, v.shape, 1) // HEAD_DIM
    for h in range(N_SUB):
        kbd_scr[:, h * mem_len:(h + 1) * mem_len] = jnp.where(row_head == h, kt, 0.0).astype(BF16)
        vbd_scr[h * mem_len:(h + 1) * mem_len, :] = jnp.where(col_head == h, v, 0.0).astype(BF16)


def _layer_kernel(x0_ref, xn_ref, mem_ref, w_in_hbm, wkt_ref, wv_ref, conv_a_ref, sg_g_ref, sg_b_ref, sg_w_ref,
                  sg_bias_ref, pool_w_ref, pool_scale_ref, dw_w_ref, dw_b_ref, cc_g_ref, cc_b_ref,
                  pw_ref, w_out_ref, ln_g_ref, ln_b_ref, o_ref, w_in_ref, w_stage, w_sem, ring_scr, xb_scr, hcat_scr, kbd_scr, vbd_scr,
                  cx_scr, xc_scr, h_scr, *proj_scrs, tile, n_seq, layer, alpha):
    g = pl.program_id(0)
    last = pl.num_programs(0) - 1
    s = g % n_seq
    proj_of = dict(zip(PROJ_GROUPS, proj_scrs))

    def row(ref):
        return ref[layer:layer + 1, :]

    def proj_tile(lo):
        group = [grp for grp in PROJ_GROUPS if grp[0] <= lo < grp[0] + grp[1]][0]
        return proj_of[group], lo - group[0]

    def w_in_copy(c):
        return pltpu.make_async_copy(
            w_in_hbm.at[layer, :, pl.ds(c * W_STAGE_COLS, W_STAGE_COLS)], w_stage.at[c % 2], w_sem.at[c % 2])

    @pl.when(g == 0)
    def _():
        n_chunks = D_IN // W_STAGE_COLS
        w_in_copy(0).start()
        x0 = x0_ref[0]
        ring_scr[0] = x0
        ring_scr[X_RING - 1] = jnp.zeros_like(x0)
        xb0 = x0.astype(BF16)
        for c in range(n_chunks + 1):
            if c + 1 < n_chunks:
                w_in_copy(c + 1).start()
            if c < n_chunks:
                w_in_copy(c).wait()
                w_in_ref[:, c * W_STAGE_COLS:(c + 1) * W_STAGE_COLS] = w_stage[c % 2].astype(BF16)
            if c:
                lo = (c - 1) * W_STAGE_COLS
                res = _dot(xb0, w_in_ref[:, lo:lo + W_STAGE_COLS])
                for t in range(0, W_STAGE_COLS, D_G):
                    dst, off = proj_tile(lo + t)
                    dst[:, off:off + D_G] = res[:, t:t + D_G]
        hcat_scr[...] = jnp.zeros(hcat_scr.shape, BF16)

    @pl.when(s == 0)
    def _():
        _project_memory(mem_ref, wkt_ref, wv_ref, kbd_scr, vbd_scr)
        cx_scr[0:HALO_A, :] = jnp.zeros((HALO_A, D_G), F32)
        xc_scr[0:HALO_C, :] = jnp.zeros((HALO_C, D_G), F32)
        h_scr[0:HALO_D, :] = jnp.zeros((HALO_D, D_G), F32)

    def out_matmul():
        o_ref[0] = lax.dot_general(hcat_scr[...], w_out_ref[...], (((1,), (1,)), ((), ())),
                                   preferred_element_type=F32)

    def out_norm(alpha_x, rows=slice(None)):
        x_prev = ring_scr[(g + X_RING - 1) % X_RING, rows, :]
        o_ref[0, rows, :] = _layer_norm(alpha_x * x_prev + o_ref[0, rows, :],
                                        row(ln_g_ref), row(ln_b_ref))

    @pl.when(g < last)
    def _():
        def take(lo, width):
            scr, off = proj_tile(lo)
            return scr[:, off:off + width]

        refillable = []
        refill_done = []

        def refill(*tiles):
            refillable.extend(tiles)
            while len(refillable) >= 2:
                pair = (refillable.pop(0), refillable.pop(0))
                w = jnp.concatenate([w_in_ref[:, t:t + D_G] for t in pair], axis=1)
                res = _dot(xb_scr[...], w)
                for i, t in enumerate(pair):
                    dst, dst_off = proj_tile(t)
                    dst[:, dst_off:dst_off + D_G] = res[:, i * D_G:(i + 1) * D_G]
                refill_done.append(res[tile - SUBLANES:, 2 * D_G - LANES:])

        def gate(lo):
            return jax.nn.silu(take(OFF_GATE + lo, D_G))

        xn = xn_ref[0]
        xb_scr[...] = xn.astype(BF16)
        ring_scr[(g + 1) % X_RING] = xn

        out_matmul()

        pd = take(OFF_D, 2 * D_G)
        h_scr[HALO_D:HALO_D + tile, :] = pd[:, :D_G] * jax.nn.sigmoid(pd[:, D_G:])

        pa = take(OFF_A, 3 * D_G)
        xa, ba, ca = pa[:, :D_G], pa[:, D_G:2 * D_G], pa[:, 2 * D_G:]
        cx = ca * xa
        cx_scr[HALO_A:HALO_A + tile, :] = cx
        conv_a = conv_a_ref[2:3, :] * cx
        for k in range(CONV_A - 1):
            off = HALO_A - (CONV_A - 1) + k
            conv_a = conv_a + conv_a_ref[k:k + 1, :] * cx_scr[off:off + tile, :]
        h_a = (ba * conv_a * gate(0)).astype(BF16)
        cx_scr[0:HALO_A, :] = cx_scr[tile:tile + HALO_A, :]
        q = take(OFF_E, D_G).astype(BF16)

        lead = HALO_D - (CONV_D - 1)
        conv_d = []
        for c in range(tile // CONV_ROWS):
            base = c * CONV_ROWS
            bias = jnp.broadcast_to(row(dw_b_ref), (SUBLANES, D_G))
            if len(conv_d) >= CONV_CHAINS:
                bias = _sequenced_after(bias, conv_d[-CONV_CHAINS])
            acc = jnp.concatenate([bias] * (CONV_ROWS // SUBLANES), axis=0)
            for r in range(SUBLANES):
                rows = CONV_ROWS + (SUBLANES if r else 0)
                part = None
                for j in range(r, lead + CONV_D, SUBLANES):
                    if j < lead:
                        continue
                    term = dw_w_ref[j - lead:j - lead + 1, :] * h_scr[base + j - r:base + j - r + rows, :]
                    part = term if part is None else part + term
                acc = acc + part[r:r + CONV_ROWS, :]
            conv_d.append(acc)
        h_scr[0:HALO_D, :] = h_scr[tile:tile + HALO_D, :]
        refill(OFF_D, OFF_D + D_G, OFF_A, OFF_A + D_G)
        hn = jax.nn.silu(_layer_norm(jnp.concatenate(conv_d, axis=0), row(cc_g_ref), row(cc_b_ref)))
        y_d = _dot(hn.astype(BF16), pw_ref[...])
        scores = _dot(q, kbd_scr[...])
        refill(OFF_A + 2 * D_G, OFF_GATE)
        h_d = (y_d * gate(3 * D_G)).astype(BF16)

        mem_len = scores.shape[1] // N_SUB
        probs = []
        for h in range(N_SUB):
            sh = scores[:, h * mem_len:(h + 1) * mem_len]
            e = jnp.exp(sh - jnp.max(sh, axis=-1, keepdims=True))
            inv = 1.0 / jnp.sum(e, axis=-1, keepdims=True)
            probs.append((e * inv).astype(BF16))
        refill(OFF_GATE + 3 * D_G, OFF_E)
        y_e = _dot(jnp.concatenate(probs, axis=1), vbd_scr[...])

        pb = take(OFF_B, 2 * D_G)
        u = jax.nn.gelu(pb[:, :D_G])
        v = _layer_norm(jax.nn.gelu(pb[:, D_G:]), row(sg_g_ref), row(sg_b_ref))
        vb = v.astype(BF16)
        refill(OFF_B, OFF_B + D_G)
        wt = lax.broadcasted_iota(jnp.int32, (CHUNK, N_SUB * CHUNK), 0)
        ws = lax.broadcasted_iota(jnp.int32, (CHUNK, N_SUB * CHUNK), 1) % CHUNK
        w_mix = jnp.where(ws <= wt, sg_w_ref[...], 0.0).astype(BF16)
        lane_head = lax.broadcasted_iota(jnp.int32, (CHUNK, D_G), 1) // HEAD_DIM
        mixed = []
        for c in range(tile // CHUNK):
            vc = vb[c * CHUNK:(c + 1) * CHUNK, :]
            rhs = jnp.concatenate(
                [jnp.where(lane_head == h, vc, jnp.zeros_like(vc)) for h in range(N_SUB)], axis=0)
            mixed.append(_dot(w_mix, rhs) + sg_bias_ref[...])
        y_b = u * jnp.concatenate(mixed, axis=0)

        xc = take(OFF_C, D_G)
        xc_scr[HALO_C:HALO_C + tile, :] = xc

        xe = xc_scr[...]

        def doubled(val, steps):
            for shift in steps:
                val = val + pltpu.roll(val, shift, axis=0)
            return val

        s2 = doubled(xe[:, :2 * HEAD_DIM], (1,))
        s4 = doubled(s2, (2,))
        s8 = doubled(xe[:, 2 * HEAD_DIM:], (1, 2, 4))
        s16 = doubled(s8, (8,))
        lane = lax.broadcasted_iota(jnp.int32, s2.shape, 1)
        win_sum = jnp.concatenate([jnp.where(lane < HEAD_DIM, s2, s4),
                                   jnp.where(lane < HEAD_DIM, s8, s16)], axis=1)[HALO_C:, :]
        t_abs = s * tile + lax.broadcasted_iota(jnp.int32, (HALO_C, D_G), 0)
        window = jnp.left_shift(2, lax.broadcasted_iota(jnp.int32, (HALO_C, D_G), 1) // HEAD_DIM)
        count = jnp.minimum(t_abs + 1, window).astype(F32)
        inv_window = 1.0 / window[0:1, :].astype(F32)
        mean = jnp.concatenate([win_sum[:HALO_C, :] / count, win_sum[HALO_C:, :] * inv_window], axis=0)
        pooled = (mean - xc).astype(BF16)
        y_c = _dot(pooled, pool_w_ref[...]) * row(pool_scale_ref)
        xc_scr[0:HALO_C, :] = xc_scr[tile:tile + HALO_C, :]

        h_e = (y_e * gate(4 * D_G)).astype(BF16)
        refill(OFF_C, OFF_GATE + 4 * D_G)
        h_b = (y_b * gate(D_G)).astype(BF16)
        h_c = (y_c * gate(2 * D_G)).astype(BF16)
        refill(OFF_GATE + D_G, OFF_GATE + 2 * D_G)
        assert not refillable
        hcat_scr[...] = jnp.concatenate([h_a, h_b, h_c, h_d, h_e], axis=1)

        rows = tile // OUT_NORM_CHUNKS
        for c in range(OUT_NORM_CHUNKS):
            alpha1 = _sequenced_after(jnp.full((SUBLANES, LANES), alpha, F32), refill_done[OUT_NORM_FIRST_REFILL + c])
            alpha_t = jnp.concatenate([jnp.concatenate([alpha1] * (D_MODEL // LANES), axis=1)] * (rows // SUBLANES), axis=0)
            out_norm(alpha_t, slice(c * rows, (c + 1) * rows))

    @pl.when(g == last)
    def _():
        out_matmul()
        out_norm(alpha)


def _layer_spec(shape, layer):
    resident = dict(pipeline_mode=pl.Buffered(1))
    if len(shape) == 2:
        return pl.BlockSpec(tuple(shape), lambda g: (0, 0), **resident)
    return pl.BlockSpec((None,) + tuple(shape[1:]), lambda g: (layer,) + (0,) * (len(shape) - 1), **resident)


def _layer_call(x, mem, w_in, params, layer, alpha):
    batch, seq, _ = x.shape
    tile = SEQ_TILE
    assert seq % tile == 0 and tile % CHUNK == 0 and tile % CONV_ROWS == 0
    assert POOL_WINDOWS == tuple(2 << grp for grp in range(N_SUB)) and 2 * HEAD_DIM == LANES
    n_seq = seq // tile
    n_tiles = batch * n_seq
    mem_len = mem.shape[1]

    def tile_at(g):
        g = jnp.clip(g, 0, n_tiles - 1)
        return g // n_seq, g % n_seq

    in_specs = [
        pl.BlockSpec((1, tile, D_MODEL), lambda g: (0, 0, 0), pipeline_mode=pl.Buffered(1)),
        pl.BlockSpec((1, tile, D_MODEL), lambda g: (*tile_at(g + 1), 0)),
        pl.BlockSpec((1, mem_len, D_MODEL), lambda g: (tile_at(g)[0], 0, 0)),
        pl.BlockSpec(memory_space=pl.ANY),
    ] + [_layer_spec(p.shape, layer) for p in params]
    return pl.pallas_call(
        functools.partial(_layer_kernel, tile=tile, n_seq=n_seq, layer=layer, alpha=alpha),
        grid=(n_tiles + 1,),
        in_specs=in_specs,
        out_specs=pl.BlockSpec((1, tile, D_MODEL), lambda g: (*tile_at(g - 1), 0)),
        out_shape=jax.ShapeDtypeStruct(x.shape, x.dtype),
        scratch_shapes=[
            pltpu.VMEM((D_MODEL, D_IN), BF16),
            pltpu.VMEM((2, D_MODEL, W_STAGE_COLS), F32),
            pltpu.SemaphoreType.DMA((2,)),
            pltpu.VMEM((X_RING, tile, D_MODEL), F32),
            pltpu.VMEM((tile, D_MODEL), BF16),
            pltpu.VMEM((tile, D_MIX), BF16),
            pltpu.VMEM((D_G, N_SUB * mem_len), BF16),
            pltpu.VMEM((N_SUB * mem_len, D_G), BF16),
            pltpu.VMEM((HALO_A + tile, D_G), F32),
            pltpu.VMEM((HALO_C + tile, D_G), F32),
            pltpu.VMEM((HALO_D + tile, D_G), F32),
        ] + [pltpu.VMEM((tile, width), F32) for _, width in PROJ_GROUPS],
        compiler_params=pltpu.CompilerParams(
            dimension_semantics=("arbitrary",),
            vmem_limit_bytes=VMEM_LIMIT_BYTES),
        name="mixer_layer",
    )(x, x, mem, w_in, *params)


def kernel(x, mem, w_in, conv_a_w, sg_ln_g, sg_ln_b, sg_w, sg_b, pool_w, pool_scale, cc_dw_w, cc_dw_b, cc_ln_g, cc_ln_b, cc_pw_w, w_kv, w_out, ln_g, ln_b):
    depth = w_in.shape[0]
    alpha = (2.0 * depth) ** 0.25
    wkt = w_kv[:, :, :D_G].transpose(0, 2, 1).astype(BF16)
    wv = w_kv[:, :, D_G:].astype(BF16)
    sg_w_cat = sg_w.transpose(0, 2, 1, 3).reshape(depth, CHUNK, N_SUB * CHUNK)
    sg_bias = jnp.repeat(sg_b.transpose(0, 2, 1), HEAD_DIM, axis=2)
    pool_bd = jnp.einsum("gh,lgcd->lgchd", jnp.eye(N_SUB, dtype=pool_w.dtype), pool_w)
    pool_bd = pool_bd.reshape(depth, D_G, D_G).astype(BF16)
    params = (
        wkt, wv, conv_a_w, sg_ln_g, sg_ln_b, sg_w_cat, sg_bias,
        pool_bd, pool_scale, cc_dw_w, cc_dw_b, cc_ln_g, cc_ln_b,
        cc_pw_w.astype(BF16), w_out.transpose(0, 2, 1).astype(BF16), ln_g, ln_b,
    )
    for layer in range(depth):
        x = _layer_call(x, mem, w_in, params, layer, alpha)
    return x
```

```python
import functools

import jax
import jax.numpy as jnp
from jax import lax
from jax.experimental import pallas as pl
from jax.experimental.pallas import tpu as pltpu

D_MODEL = 1024
D_G = D_MODEL // 4
N_SUB = 4
HEAD_DIM = D_G // N_SUB
D_MIX = 5 * D_G
D_IN = 9 * D_G + D_MIX
CHUNK = 128
CONV_A = 3
CONV_D = 31
POOL_WINDOWS = (2, 4, 8, 16)
LN_EPS = 1e-5

SUBLANES = 8
LANES = 128

OFF_A = 0
OFF_B = 3 * D_G
OFF_C = 5 * D_G
OFF_D = 6 * D_G
OFF_E = 8 * D_G
OFF_GATE = 9 * D_G

HALO_A = 8
HALO_C = 16
HALO_D = 32
W_STAGE_COLS = 2 * D_G
CONV_ROWS = 32
CONV_CHAINS = 2
OUT_NORM_CHUNKS = 4
OUT_NORM_FIRST_REFILL = 2

PROJ_GROUPS = ((OFF_A, 3 * D_G), (OFF_B, 2 * D_G), (OFF_C, D_G), (OFF_D, 2 * D_G), (OFF_E, D_G)) + tuple(
    (OFF_GATE + g * D_G, D_G) for g in range(5))

SEQ_TILE = 512
VMEM_LIMIT_BYTES = 56 * 1024 * 1024

F32 = jnp.float32
BF16 = jnp.bfloat16


def _dot(a, b):
    return jnp.dot(a, b, preferred_element_type=F32)


def _layer_norm(x, g, b):
    mu = jnp.mean(x, axis=-1, keepdims=True)
    xc = x - mu
    var = jnp.mean(xc * xc, axis=-1, keepdims=True)
    return xc * lax.rsqrt(var + LN_EPS) * g + b


def _sequenced_after(x, done):
    words = pltpu.bitcast(done, jnp.uint32)
    word = words[0:SUBLANES, 0:LANES]
    for i in range(0, words.shape[0], SUBLANES):
        for j in range(0, words.shape[1], LANES):
            if i or j:
                word = word + words[i:i + SUBLANES, j:j + LANES]
    word = jnp.max(word.astype(jnp.int32), axis=-1, keepdims=True).astype(jnp.uint32)
    zero = lax.shift_right_logical(lax.shift_right_logical(word, jnp.uint32(16)), jnp.uint32(16))
    return pltpu.bitcast(pltpu.bitcast(x, jnp.uint32) | zero, F32)


def _project_memory(mem_ref, wkt_ref, wv_ref, kbd_scr, vbd_scr):
    m = mem_ref[0].astype(BF16)
    mem_len = m.shape[0]
    kt = lax.dot_general(wkt_ref[...], m, (((1,), (1,)), ((), ())),
                         preferred_element_type=F32)
    kt = kt * (1.0 / (HEAD_DIM ** 0.5))
    v = _dot(m, wv_ref[...])
    row_head = lax.broadcasted_iota(jnp.int32, kt.shape, 0) // HEAD_DIM
    col_head = lax.broadcasted_iota(jnp.int32, v.shape, 1) // HEAD_DIM
    for h in range(N_SUB):
        kbd_scr[:, h * mem_len:(h + 1) * mem_len] = jnp.where(row_head == h, kt, 0.0).astype(BF16)
        vbd_scr[h * mem_len:(h + 1) * mem_len, :] = jnp.where(col_head == h, v, 0.0).astype(BF16)


def _layer_kernel(xp_ref, xn_ref, mem_ref, w_in_hbm, wkt_ref, wv_ref, conv_a_ref, sg_g_ref, sg_b_ref, sg_w_ref,
                  sg_bias_ref, pool_w_ref, pool_scale_ref, dw_w_ref, dw_b_ref, cc_g_ref, cc_b_ref,
                  pw_ref, w_out_ref, ln_g_ref, ln_b_ref, o_ref, w_in_ref, w_stage, w_sem, xb_scr, hcat_scr, kbd_scr, vbd_scr,
                  cx_scr, xc_scr, h_scr, *proj_scrs, tile, n_seq, layer, alpha):
    g = pl.program_id(0)
    last = pl.num_programs(0) - 1
    s = g % n_seq
    proj_of = dict(zip(PROJ_GROUPS, proj_scrs))

    def row(ref):
        return ref[layer:layer + 1, :]

    def proj_tile(lo):
        group = [grp for grp in PROJ_GROUPS if grp[0] <= lo < grp[0] + grp[1]][0]
        return proj_of[group], lo - group[0]

    def w_in_copy(c):
        return pltpu.make_async_copy(
            w_in_hbm.at[layer, :, pl.ds(c * W_STAGE_COLS, W_STAGE_COLS)], w_stage.at[c % 2], w_sem.at[c % 2])

    @pl.when(g == 0)
    def _():
        n_chunks = D_IN // W_STAGE_COLS
        w_in_copy(0).start()
        xb0 = xp_ref[0].astype(BF16)
        for c in range(n_chunks + 1):
            if c + 1 < n_chunks:
                w_in_copy(c + 1).start()
            if c < n_chunks:
                w_in_copy(c).wait()
                w_in_ref[:, c * W_STAGE_COLS:(c + 1) * W_STAGE_COLS] = w_stage[c % 2].astype(BF16)
            if c:
                lo = (c - 1) * W_STAGE_COLS
                res = _dot(xb0, w_in_ref[:, lo:lo + W_STAGE_COLS])
                for t in range(0, W_STAGE_COLS, D_G):
                    dst, off = proj_tile(lo + t)
                    dst[:, off:off + D_G] = res[:, t:t + D_G]
        hcat_scr[...] = jnp.zeros(hcat_scr.shape, BF16)

    @pl.when(s == 0)
    def _():
        _project_memory(mem_ref, wkt_ref, wv_ref, kbd_scr, vbd_scr)
        cx_scr[0:HALO_A, :] = jnp.zeros((HALO_A, D_G), F32)
        xc_scr[0:HALO_C, :] = jnp.zeros((HALO_C, D_G), F32)
        h_scr[0:HALO_D, :] = jnp.zeros((HALO_D, D_G), F32)

    def out_matmul():
        o_ref[0] = lax.dot_general(hcat_scr[...], w_out_ref[...], (((1,), (1,)), ((), ())),
                                   preferred_element_type=F32)

    def out_norm(alpha_x, rows=slice(None)):
        o_ref[0, rows, :] = _layer_norm(alpha_x * xp_ref[0, rows, :] + o_ref[0, rows, :],
                                        row(ln_g_ref), row(ln_b_ref))

    @pl.when(g < last)
    def _():
        def take(lo, width):
            scr, off = proj_tile(lo)
            return scr[:, off:off + width]

        refillable = []
        refill_done = []

        def refill(*tiles):
            refillable.extend(tiles)
            while len(refillable) >= 2:
                pair = (refillable.pop(0), refillable.pop(0))
                w = jnp.concatenate([w_in_ref[:, t:t + D_G] for t in pair], axis=1)
                res = _dot(xb_scr[...], w)
                for i, t in enumerate(pair):
                    dst, dst_off = proj_tile(t)
                    dst[:, dst_off:dst_off + D_G] = res[:, i * D_G:(i + 1) * D_G]
                refill_done.append(res[tile - SUBLANES:, 2 * D_G - LANES:])

        def gate(lo):
            return jax.nn.silu(take(OFF_GATE + lo, D_G))

        xb_scr[...] = xn_ref[0].astype(BF16)

        out_matmul()

        pd = take(OFF_D, 2 * D_G)
        h_scr[HALO_D:HALO_D + tile, :] = pd[:, :D_G] * jax.nn.sigmoid(pd[:, D_G:])

        pa = take(OFF_A, 3 * D_G)
        xa, ba, ca = pa[:, :D_G], pa[:, D_G:2 * D_G], pa[:, 2 * D_G:]
        cx = ca * xa
        cx_scr[HALO_A:HALO_A + tile, :] = cx
        conv_a = conv_a_ref[2:3, :] * cx
        for k in range(CONV_A - 1):
            off = HALO_A - (CONV_A - 1) + k
            conv_a = conv_a + conv_a_ref[k:k + 1, :] * cx_scr[off:off + tile, :]
        h_a = (ba * conv_a * gate(0)).astype(BF16)
        cx_scr[0:HALO_A, :] = cx_scr[tile:tile + HALO_A, :]
        q = take(OFF_E, D_G).astype(BF16)

        lead = HALO_D - (CONV_D - 1)
        conv_d = []
        halves = []
        for c in range(tile // CONV_ROWS):
            base = c * CONV_ROWS
            cols = []
            for lo in (0, LANES):
                bias = jnp.broadcast_to(row(dw_b_ref)[:, lo:lo + LANES], (SUBLANES, LANES))
                if len(halves) >= 2 * CONV_CHAINS:
                    bias = _sequenced_after(bias, halves[-2 * CONV_CHAINS])
                acc = jnp.concatenate([bias] * (CONV_ROWS // SUBLANES), axis=0)
                for r in range(SUBLANES):
                    rows = CONV_ROWS + (SUBLANES if r else 0)
                    part = None
                    for j in range(r, lead + CONV_D, SUBLANES):
                        if j < lead:
                            continue
                        term = dw_w_ref[j - lead:j - lead + 1, lo:lo + LANES] * h_scr[base + j - r:base + j - r + rows, lo:lo + LANES]
                        part = term if part is None else part + term
                    acc = acc + part[r:r + CONV_ROWS, :]
                halves.append(acc)
                cols.append(acc)
            conv_d.append(jnp.concatenate(cols, axis=1))
        h_scr[0:HALO_D, :] = h_scr[tile:tile + HALO_D, :]
        refill(OFF_D, OFF_D + D_G, OFF_A, OFF_A + D_G)
        hn = jax.nn.silu(_layer_norm(jnp.concatenate(conv_d, axis=0), row(cc_g_ref), row(cc_b_ref)))
        y_d = _dot(hn.astype(BF16), pw_ref[...])
        scores = _dot(q, kbd_scr[...])
        refill(OFF_A + 2 * D_G, OFF_GATE)
        h_d = (y_d * gate(3 * D_G)).astype(BF16)

        mem_len = scores.shape[1] // N_SUB
        probs = []
        for h in range(N_SUB):
            sh = scores[:, h * mem_len:(h + 1) * mem_len]
            e = jnp.exp(sh - jnp.max(sh, axis=-1, keepdims=True))
            inv = 1.0 / jnp.sum(e, axis=-1, keepdims=True)
            probs.append((e * inv).astype(BF16))
        refill(OFF_GATE + 3 * D_G, OFF_E)
        y_e = _dot(jnp.concatenate(probs, axis=1), vbd_scr[...])

        pb = take(OFF_B, 2 * D_G)
        u = jax.nn.gelu(pb[:, :D_G])
        v = _layer_norm(jax.nn.gelu(pb[:, D_G:]), row(sg_g_ref), row(sg_b_ref))
        vb = v.astype(BF16)
        refill(OFF_B, OFF_B + D_G)
        wt = lax.broadcasted_iota(jnp.int32, (CHUNK, N_SUB * CHUNK), 0)
        ws = lax.broadcasted_iota(jnp.int32, (CHUNK, N_SUB * CHUNK), 1) % CHUNK
        w_mix = jnp.where(ws <= wt, sg_w_ref[...], 0.0).astype(BF16)
        lane_head = lax.broadcasted_iota(jnp.int32, (CHUNK, D_G), 1) // HEAD_DIM
        mixed = []
        for c in range(tile // CHUNK):
            vc = vb[c * CHUNK:(c + 1) * CHUNK, :]
            rhs = jnp.concatenate(
                [jnp.where(lane_head == h, vc, jnp.zeros_like(vc)) for h in range(N_SUB)], axis=0)
            mixed.append(_dot(w_mix, rhs) + sg_bias_ref[...])
        y_b = u * jnp.concatenate(mixed, axis=0)

        xc = take(OFF_C, D_G)
        xc_scr[HALO_C:HALO_C + tile, :] = xc

        xe = xc_scr[...]

        def doubled(val, steps):
            for shift in steps:
                val = val + pltpu.roll(val, shift, axis=0)
            return val

        s2 = doubled(xe[:, :2 * HEAD_DIM], (1,))
        s4 = doubled(s2, (2,))
        s8 = doubled(xe[:, 2 * HEAD_DIM:], (1, 2, 4))
        s16 = doubled(s8, (8,))
        lane = lax.broadcasted_iota(jnp.int32, s2.shape, 1)
        win_sum = jnp.concatenate([jnp.where(lane < HEAD_DIM, s2, s4),
                                   jnp.where(lane < HEAD_DIM, s8, s16)], axis=1)[HALO_C:, :]
        t_abs = s * tile + lax.broadcasted_iota(jnp.int32, (HALO_C, D_G), 0)
        window = jnp.left_shift(2, lax.broadcasted_iota(jnp.int32, (HALO_C, D_G), 1) // HEAD_DIM)
        count = jnp.minimum(t_abs + 1, window).astype(F32)
        inv_window = 1.0 / window[0:1, :].astype(F32)
        mean = jnp.concatenate([win_sum[:HALO_C, :] / count, win_sum[HALO_C:, :] * inv_window], axis=0)
        pooled = (mean - xc).astype(BF16)
        y_c = _dot(pooled, pool_w_ref[...]) * row(pool_scale_ref)
        xc_scr[0:HALO_C, :] = xc_scr[tile:tile + HALO_C, :]

        h_e = (y_e * gate(4 * D_G)).astype(BF16)
        refill(OFF_C, OFF_GATE + 4 * D_G)
        h_b = (y_b * gate(D_G)).astype(BF16)
        h_c = (y_c * gate(2 * D_G)).astype(BF16)
        refill(OFF_GATE + D_G, OFF_GATE + 2 * D_G)
        assert not refillable
        hcat_scr[...] = jnp.concatenate([h_a, h_b, h_c, h_d, h_e], axis=1)

        rows = tile // OUT_NORM_CHUNKS
        for c in range(OUT_NORM_CHUNKS):
            alpha1 = _sequenced_after(jnp.full((SUBLANES, LANES), alpha, F32), refill_done[OUT_NORM_FIRST_REFILL + c])
            alpha_t = jnp.concatenate([jnp.concatenate([alpha1] * (D_MODEL // LANES), axis=1)] * (rows // SUBLANES), axis=0)
            out_norm(alpha_t, slice(c * rows, (c + 1) * rows))

    @pl.when(g == last)
    def _():
        out_matmul()
        out_norm(alpha)


def _layer_spec(shape, layer):
    resident = dict(pipeline_mode=pl.Buffered(1))
    if len(shape) == 2:
        return pl.BlockSpec(tuple(shape), lambda g: (0, 0), **resident)
    return pl.BlockSpec((None,) + tuple(shape[1:]), lambda g: (layer,) + (0,) * (len(shape) - 1), **resident)


def _layer_call(x, mem, w_in, params, layer, alpha):
    batch, seq, _ = x.shape
    tile = SEQ_TILE
    assert seq % tile == 0 and tile % CHUNK == 0 and tile % CONV_ROWS == 0
    assert POOL_WINDOWS == tuple(2 << grp for grp in range(N_SUB)) and 2 * HEAD_DIM == LANES
    n_seq = seq // tile
    n_tiles = batch * n_seq
    mem_len = mem.shape[1]

    def tile_at(g):
        g = jnp.clip(g, 0, n_tiles - 1)
        return g // n_seq, g % n_seq

    in_specs = [
        pl.BlockSpec((1, tile, D_MODEL), lambda g: (*tile_at(g - 1), 0)),
        pl.BlockSpec((1, tile, D_MODEL), lambda g: (*tile_at(g + 1), 0)),
        pl.BlockSpec((1, mem_len, D_MODEL), lambda g: (tile_at(g)[0], 0, 0)),
        pl.BlockSpec(memory_space=pl.ANY),
    ] + [_layer_spec(p.shape, layer) for p in params]
    return pl.pallas_call(
        functools.partial(_layer_kernel, tile=tile, n_seq=n_seq, layer=layer, alpha=alpha),
        grid=(n_tiles + 1,),
        in_specs=in_specs,
        out_specs=pl.BlockSpec((1, tile, D_MODEL), lambda g: (*tile_at(g - 1), 0)),
        out_shape=jax.ShapeDtypeStruct(x.shape, x.dtype),
        scratch_shapes=[
            pltpu.VMEM((D_MODEL, D_IN), BF16),
            pltpu.VMEM((2, D_MODEL, W_STAGE_COLS), F32),
            pltpu.SemaphoreType.DMA((2,)),
            pltpu.VMEM((tile, D_MODEL), BF16),
            pltpu.VMEM((tile, D_MIX), BF16),
            pltpu.VMEM((D_G, N_SUB * mem_len), BF16),
            pltpu.VMEM((N_SUB * mem_len, D_G), BF16),
            pltpu.VMEM((HALO_A + tile, D_G), F32),
            pltpu.VMEM((HALO_C + tile, D_G), F32),
            pltpu.VMEM((HALO_D + tile, D_G), F32),
        ] + [pltpu.VMEM((tile, width), F32) for _, width in PROJ_GROUPS],
        compiler_params=pltpu.CompilerParams(
            dimension_semantics=("arbitrary",),
            vmem_limit_bytes=VMEM_LIMIT_BYTES),
        name="mixer_layer",
    )(x, x, mem, w_in, *params)


def kernel(x, mem, w_in, conv_a_w, sg_ln_g, sg_ln_b, sg_w, sg_b, pool_w, pool_scale, cc_dw_w, cc_dw_b, cc_ln_g, cc_ln_b, cc_pw_w, w_kv, w_out, ln_g, ln_b):
    depth = w_in.shape[0]
    alpha = (2.0 * depth) ** 0.25
    wkt = w_kv[:, :, :D_G].transpose(0, 2, 1).astype(BF16)
    wv = w_kv[:, :, D_G:].astype(BF16)
    sg_w_cat = sg_w.transpose(0, 2, 1, 3).reshape(depth, CHUNK, N_SUB * CHUNK)
    sg_bias = jnp.repeat(sg_b.transpose(0, 2, 1), HEAD_DIM, axis=2)
    pool_bd = jnp.einsum("gh,lgcd->lgchd", jnp.eye(N_SUB, dtype=pool_w.dtype), pool_w)
    pool_bd = pool_bd.reshape(depth, D_G, D_G).astype(BF16)
    params = (
        wkt, wv, conv_a_w, sg_ln_g, sg_ln_b, sg_w_cat, sg_bias,
        pool_bd, pool_scale, cc_dw_w, cc_dw_b, cc_ln_g, cc_ln_b,
        cc_pw_w.astype(BF16), w_out.transpose(0, 2, 1).astype(BF16), ln_g, ln_b,
    )
    for layer in range(depth):
        x = _layer_call(x, mem, w_in, params, layer, alpha)
    return x
```
